```python
import math
import jax, jax.numpy as jnp
from jax import lax
import numpy as np


D_MODEL = 1024
BATCH = 8
SEQ = 2048
DEPTH = 4
DEC_BATCH = 32
DEC_SEQ = 8
PAST_LEN = 8192
PAGE_SIZE = 128

MIX_WIDTH = 3 * D_MODEL // 2
GROUP_WIDTH = MIX_WIDTH // 3
SSD_WIDTH = GROUP_WIDTH
SSD_HEAD_DIM = 64
SSD_HEADS = SSD_WIDTH // SSD_HEAD_DIM
SSD_GROUPS = 2
SSD_STATE = 64
SSD_CONV = 4
SSD_CHUNK = 128
CONV_DIM = SSD_WIDTH + 2 * SSD_GROUPS * SSD_STATE
POOL_WIDTH = GROUP_WIDTH
POOL_WINDOWS = (2, 4, 8, 16)
POOL_GROUP = POOL_WIDTH // len(POOL_WINDOWS)
POOL_BUF = max(POOL_WINDOWS) - 1
ATTN_WIDTH = GROUP_WIDTH
ATTN_HEAD_DIM = 64
ATTN_HEADS = ATTN_WIDTH // (2 * ATTN_HEAD_DIM)
Q_BLOCK = 128
ROPE_THETA = 10000.0
FFN_HIDDEN = -(-(8 * D_MODEL) // (3 * 256)) * 256
RMS_EPS = 1e-6
SPLIT_SIZES = (SSD_WIDTH, CONV_DIM, SSD_HEADS, POOL_WIDTH, ATTN_WIDTH, ATTN_WIDTH, ATTN_WIDTH)
SPLIT_POINTS = (SSD_WIDTH, SSD_WIDTH + CONV_DIM, SSD_WIDTH + CONV_DIM + SSD_HEADS,
                SSD_WIDTH + CONV_DIM + SSD_HEADS + POOL_WIDTH,
                SSD_WIDTH + CONV_DIM + SSD_HEADS + POOL_WIDTH + ATTN_WIDTH,
                SSD_WIDTH + CONV_DIM + SSD_HEADS + POOL_WIDTH + 2 * ATTN_WIDTH)
IN_PROJ = SSD_WIDTH + CONV_DIM + SSD_HEADS + POOL_WIDTH + 3 * ATTN_WIDTH

kernel_name = 'hymba_ssd_pool_diffattn_step'


def rms_norm(x, g, eps=RMS_EPS):
    xf = x.astype(jnp.float32)
    y = xf * lax.rsqrt(jnp.mean(xf * xf, axis=-1, keepdims=True) + eps)
    return (y * g.astype(jnp.float32)).astype(x.dtype)


def apply_rope(x, pos):
    half = ATTN_HEAD_DIM // 2
    inv_freq = ROPE_THETA ** (-jnp.arange(half, dtype=jnp.float32) / half)
    ang = pos.astype(jnp.float32)[:, None] * inv_freq[None, :]
    cos = jnp.cos(ang)[:, None, None, :]
    sin = jnp.sin(ang)[:, None, None, :]
    xf = x.astype(jnp.float32)
    x1, x2 = xf[..., :half], xf[..., half:]
    return jnp.concatenate([x1 * cos - x2 * sin, x2 * cos + x1 * sin], axis=-1).astype(x.dtype)


def ssd_scan(x, dt, a, b_mat, c_mat, h0):
    bsz, seq_len, n_h, p_dim = x.shape
    q_len = SSD_CHUNK if seq_len % SSD_CHUNK == 0 else seq_len
    n_c = seq_len // q_len
    xc = x.astype(jnp.float32).reshape(bsz, n_c, q_len, n_h, p_dim)
    bc = b_mat.astype(jnp.float32).reshape(bsz, n_c, q_len, n_h, -1)
    cc = c_mat.astype(jnp.float32).reshape(bsz, n_c, q_len, n_h, -1)
    dtc = dt.reshape(bsz, n_c, q_len, n_h)
    cum = jnp.cumsum(dtc * a, axis=2)
    seg = cum[:, :, :, None, :] - cum[:, :, None, :, :]
    causal = jnp.tril(jnp.ones((q_len, q_len), dtype=bool))[None, None, :, :, None]
    decay = jnp.where(causal, jnp.exp(jnp.where(causal, seg, 0.0)), 0.0)
    scores = jnp.einsum('bcihn,bcjhn->bcijh', cc, bc) * decay * dtc[:, :, None, :, :]
    y_diag = jnp.einsum('bcijh,bcjhp->bcihp', scores, xc)
    to_end = jnp.exp(cum[:, :, -1:, :] - cum) * dtc
    chunk_states = jnp.einsum('bclhn,bclh,bclhp->bchpn', bc, to_end, xc)
    chunk_decay = jnp.exp(cum[:, :, -1, :])

    def step(h, inp):
        dec, st = inp
        return dec[:, :, None, None] * h + st, h

    h_final, h_in = lax.scan(step, h0, (jnp.swapaxes(chunk_decay, 0, 1), jnp.swapaxes(chunk_states, 0, 1)))
    h_in = jnp.swapaxes(h_in, 0, 1)
    y_off = jnp.einsum('bclhn,bchpn,bclh->bclhp', cc, h_in, jnp.exp(cum))
    return (y_diag + y_off).reshape(bsz, seq_len, n_h, p_dim), h_final


def diff_attn_block(q, k, v, q_pos, k_pos, lam):
    s = jnp.einsum('bqhcd,bkhcd->bhcqk', q, k, preferred_element_type=jnp.float32) * (ATTN_HEAD_DIM ** -0.5)
    s = jnp.where(k_pos[None, :] <= q_pos[:, None], s, -jnp.inf)
    p = jax.nn.softmax(s, axis=-1)
    a = p[:, :, 0] - lam * p[:, :, 1]
    return jnp.einsum('bhqk,bkhe->bqhe', a.astype(v.dtype), v, preferred_element_type=jnp.float32)


def causal_diff_attention(q, k, v, q_pos, k_pos, lam):
    bsz, seq_len = q.shape[:2]
    if seq_len > Q_BLOCK and seq_len % Q_BLOCK == 0:
        n_b = seq_len // Q_BLOCK
        qb = jnp.swapaxes(q.reshape(bsz, n_b, Q_BLOCK, *q.shape[2:]), 0, 1)
        pb = q_pos.reshape(n_b, Q_BLOCK)
        ob = lax.map(lambda args: diff_attn_block(args[0], k, v, args[1], k_pos, lam), (qb, pb))
        return jnp.swapaxes(ob, 0, 1).reshape(bsz, seq_len, *ob.shape[3:])
    return diff_attn_block(q, k, v, q_pos, k_pos, lam)


def trunk_layer(h, pos, conv_buf, ssm_h0, pool_buf, past_k, past_v, past_pos, lam_init,
                norm_mix, w_in, conv_w, conv_b, dt_bias, a_log, d_skip, ssd_norm,
                pool_w, pool_scale, q_norm, k_norm, lam_q1, lam_k1, lam_q2, lam_k2, subln,
                w_out, norm_ffn, w_gate_up, w_down):
    bsz, seq_len, _ = h.shape
    u = rms_norm(h, norm_mix)
    proj = jnp.einsum('bld,de->ble', u, w_in)
    z, xbc, dt_raw, pool_in, q, k, v = jnp.split(proj, SPLIT_POINTS, axis=-1)

    ext = jnp.concatenate([conv_buf.astype(xbc.dtype), xbc], axis=1)
    conv = sum(ext[:, i:i + seq_len] * conv_w[i] for i in range(SSD_CONV)) + conv_b
    conv = jax.nn.silu(conv)
    xs, b_mat, c_mat = jnp.split(conv, (SSD_WIDTH, SSD_WIDTH + SSD_GROUPS * SSD_STATE), axis=-1)
    xs = xs.reshape(bsz, seq_len, SSD_HEADS, SSD_HEAD_DIM)
    rep = SSD_HEADS // SSD_GROUPS
    b_mat = jnp.repeat(b_mat.reshape(bsz, seq_len, SSD_GROUPS, SSD_STATE), rep, axis=2)
    c_mat = jnp.repeat(c_mat.reshape(bsz, seq_len, SSD_GROUPS, SSD_STATE), rep, axis=2)
    dt = jax.nn.softplus(dt_raw.astype(jnp.float32) + dt_bias.astype(jnp.float32))
    a = -jnp.exp(a_log.astype(jnp.float32))
    y, h_fin = ssd_scan(xs, dt, a, b_mat, c_mat, ssm_h0.astype(jnp.float32))
    y = y + d_skip.astype(jnp.float32)[:, None] * xs.astype(jnp.float32)
    y = y.reshape(bsz, seq_len, SSD_WIDTH) * jax.nn.silu(z.astype(jnp.float32))
    y_ssd = rms_norm(y, ssd_norm).astype(h.dtype)
    new_conv = ext[:, -(SSD_CONV - 1):]

    pext = jnp.concatenate([pool_buf.astype(pool_in.dtype), pool_in], axis=1)
    cs = jnp.pad(jnp.cumsum(pext.astype(jnp.float32), axis=1), ((0, 0), (1, 0), (0, 0)))
    off = POOL_BUF + 1
    pooled = []
    for g, w in enumerate(POOL_WINDOWS):
        sl = slice(g * POOL_GROUP, (g + 1) * POOL_GROUP)
        win = cs[:, off:off + seq_len, sl] - cs[:, off - w:off - w + seq_len, sl]
        cnt = jnp.minimum(w, pos + 1).astype(jnp.float32)[None, :, None]
        pooled.append(win / cnt - pool_in[..., sl].astype(jnp.float32))
    pooled = jnp.stack(pooled, axis=2)
    mixed = jnp.einsum('blgc,gce->blge', pooled, pool_w.astype(jnp.float32)).reshape(bsz, seq_len, POOL_WIDTH)
    y_pool = (mixed * pool_scale.astype(jnp.float32)).astype(h.dtype)
    new_pool = pext[:, -POOL_BUF:]

    q = q.reshape(bsz, seq_len, ATTN_HEADS, 2, ATTN_HEAD_DIM)
    k = k.reshape(bsz, seq_len, ATTN_HEADS, 2, ATTN_HEAD_DIM)
    v = v.reshape(bsz, seq_len, ATTN_HEADS, 2 * ATTN_HEAD_DIM)
    q = apply_rope(rms_norm(q, q_norm), pos)
    k = apply_rope(rms_norm(k, k_norm), pos)
    lam = (jnp.exp(jnp.sum(lam_q1.astype(jnp.float32) * lam_k1.astype(jnp.float32)))
           - jnp.exp(jnp.sum(lam_q2.astype(jnp.float32) * lam_k2.astype(jnp.float32))) + lam_init)
    if past_k is None:
        k_all, v_all, k_pos = k, v, pos
    else:
        k_all = jnp.concatenate([past_k.astype(k.dtype), k], axis=1)
        v_all = jnp.concatenate([past_v.astype(v.dtype), v], axis=1)
        k_pos = jnp.concatenate([past_pos, pos], axis=0)
    o = causal_diff_attention(q, k_all, v_all, pos, k_pos, lam)
    o = rms_norm(o, subln) * (1.0 - lam_init)
    y_attn = o.reshape(bsz, seq_len, ATTN_WIDTH).astype(h.dtype)

    mix = jnp.concatenate([y_ssd, y_pool, y_attn], axis=-1)
    h = h + jnp.einsum('blm,md->bld', mix, w_out)

    u2 = rms_norm(h, norm_ffn)
    gu = jnp.einsum('bld,df->blf', u2, w_gate_up)
    gate, up = jnp.split(gu, 2, axis=-1)
    h = h + jnp.einsum('blf,fd->bld', jax.nn.silu(gate) * up, w_down)
    return h, (k, v, new_conv, h_fin.astype(ssm_h0.dtype), new_pool)


def setup_inputs(seed: int = 0) -> dict:
    key = jax.random.key(seed)
    ks = jax.random.split(key, 32)
    f32 = jnp.float32

    def nrm(k, shape, scale):
        return jax.random.normal(k, shape, f32) * scale

    def gain(k, shape):
        return 1.0 + 0.01 * jax.random.normal(k, shape, f32)

    n_pages = PAST_LEN // PAGE_SIZE
    n_used = DEC_BATCH * n_pages
    n_pool = n_used + n_used // 4
    page_table = jax.random.permutation(ks[0], n_pool)[:n_used].reshape(DEC_BATCH, n_pages).astype(jnp.int32)

    dt0 = jnp.exp(jax.random.uniform(ks[1], (DEPTH, SSD_HEADS), f32, math.log(1e-3), math.log(1e-1)))
    return {
        'x_prompt': nrm(ks[2], (BATCH, SEQ, D_MODEL), 1.0),
        'x_sample': nrm(ks[3], (DEC_BATCH, DEC_SEQ, D_MODEL), 1.0),
        'cache_k': nrm(ks[4], (DEPTH, n_pool, PAGE_SIZE, ATTN_HEADS, 2, ATTN_HEAD_DIM), 1.0),
        'cache_v': nrm(ks[5], (DEPTH, n_pool, PAGE_SIZE, ATTN_HEADS, 2 * ATTN_HEAD_DIM), 1.0),
        'state_conv': nrm(ks[6], (DEPTH, DEC_BATCH, SSD_CONV - 1, CONV_DIM), 1.0),
        'state_ssm': nrm(ks[7], (DEPTH, DEC_BATCH, SSD_HEADS, SSD_HEAD_DIM, SSD_STATE), 0.5),
        'state_pool': nrm(ks[8], (DEPTH, DEC_BATCH, POOL_BUF, POOL_WIDTH), 1.0),
        'page_table': page_table,
        'norm_mix': gain(ks[9], (DEPTH, D_MODEL)),
        'w_in': nrm(ks[10], (DEPTH, D_MODEL, IN_PROJ), D_MODEL ** -0.5),
        'conv_w': nrm(ks[11], (DEPTH, SSD_CONV, CONV_DIM), SSD_CONV ** -0.5),
        'conv_b': nrm(ks[12], (DEPTH, CONV_DIM), 0.01),
        'dt_bias': dt0 + jnp.log(-jnp.expm1(-dt0)),
        'a_log': jnp.log(jax.random.uniform(ks[13], (DEPTH, SSD_HEADS), f32, 1.0, 16.0)),
        'd_skip': gain(ks[14], (DEPTH, SSD_HEADS)),
        'ssd_norm': gain(ks[15], (DEPTH, SSD_WIDTH)),
        'pool_w': nrm(ks[16], (DEPTH, len(POOL_WINDOWS), POOL_GROUP, POOL_GROUP), POOL_GROUP ** -0.5),
        'pool_scale': gain(ks[17], (DEPTH, POOL_WIDTH)),
        'q_norm': gain(ks[18], (DEPTH, ATTN_HEAD_DIM)),
        'k_norm': gain(ks[19], (DEPTH, ATTN_HEAD_DIM)),
        'lam_q1': nrm(ks[20], (DEPTH, ATTN_HEAD_DIM), 0.1),
        'lam_k1': nrm(ks[21], (DEPTH, ATTN_HEAD_DIM), 0.1),
        'lam_q2': nrm(ks[22], (DEPTH, ATTN_HEAD_DIM), 0.1),
        'lam_k2': nrm(ks[23], (DEPTH, ATTN_HEAD_DIM), 0.1),
        'subln': gain(ks[24], (DEPTH, 2 * ATTN_HEAD_DIM)),
        'w_out': nrm(ks[25], (DEPTH, MIX_WIDTH, D_MODEL), MIX_WIDTH ** -0.5),
        'norm_ffn': gain(ks[26], (DEPTH, D_MODEL)),
        'w_gate_up': nrm(ks[27], (DEPTH, D_MODEL, 2 * FFN_HIDDEN), D_MODEL ** -0.5),
        'w_down': nrm(ks[28], (DEPTH, FFN_HIDDEN, D_MODEL), FFN_HIDDEN ** -0.5),
    }


def reference(x_prompt, x_sample, cache_k, cache_v, state_conv, state_ssm, state_pool, page_table,
              norm_mix, w_in, conv_w, conv_b, dt_bias, a_log, d_skip, ssd_norm, pool_w, pool_scale,
              q_norm, k_norm, lam_q1, lam_k1, lam_q2, lam_k2, subln, w_out, norm_ffn, w_gate_up, w_down):
    n_seq_s = page_table.shape[0]
    pos_p = jnp.arange(SEQ, dtype=jnp.int32)
    pos_s = PAST_LEN + jnp.arange(DEC_SEQ, dtype=jnp.int32)
    past_pos = jnp.arange(PAST_LEN, dtype=jnp.int32)
    bp = x_prompt.shape[0]
    conv0 = jnp.zeros((bp, SSD_CONV - 1, CONV_DIM), x_prompt.dtype)
    ssm0 = jnp.zeros((bp, SSD_HEADS, SSD_HEAD_DIM, SSD_STATE), x_prompt.dtype)
    pool0 = jnp.zeros((bp, POOL_BUF, POOL_WIDTH), x_prompt.dtype)

    hp, hs = x_prompt, x_sample
    outs_p = [[], [], [], [], []]
    outs_s = [[], [], [], [], []]
    for l in range(DEPTH):
        lw = dict(lam_init=0.8 - 0.6 * math.exp(-0.3 * l),
                  norm_mix=norm_mix[l], w_in=w_in[l], conv_w=conv_w[l], conv_b=conv_b[l],
                  dt_bias=dt_bias[l], a_log=a_log[l], d_skip=d_skip[l], ssd_norm=ssd_norm[l],
                  pool_w=pool_w[l], pool_scale=pool_scale[l], q_norm=q_norm[l], k_norm=k_norm[l],
                  lam_q1=lam_q1[l], lam_k1=lam_k1[l], lam_q2=lam_q2[l], lam_k2=lam_k2[l],
                  subln=subln[l], w_out=w_out[l], norm_ffn=norm_ffn[l],
                  w_gate_up=w_gate_up[l], w_down=w_down[l])
        hp, st_p = trunk_layer(hp, pos_p, conv0, ssm0, pool0, None, None, None, **lw)
        for lst, s in zip(outs_p, st_p):
            lst.append(s)
        past_k = cache_k[l, page_table].reshape(n_seq_s, PAST_LEN, ATTN_HEADS, 2, ATTN_HEAD_DIM)
        past_v = cache_v[l, page_table].reshape(n_seq_s, PAST_LEN, ATTN_HEADS, 2 * ATTN_HEAD_DIM)
        hs, st_s = trunk_layer(hs, pos_s, state_conv[l], state_ssm[l], state_pool[l],
                               past_k, past_v, past_pos, **lw)
        for lst, s in zip(outs_s, st_s):
            lst.append(s)

    k_prompt, v_prompt, conv_prompt, ssm_prompt, pool_prompt = [jnp.stack(s, axis=0) for s in outs_p]
    k_sample, v_sample, conv_sample, ssm_sample, pool_sample = [jnp.stack(s, axis=0) for s in outs_s]
    return (hp, hs, k_prompt, v_prompt, conv_prompt, ssm_prompt, pool_prompt,
            k_sample, v_sample, conv_sample, ssm_sample, pool_sample)
```

```python
import functools
import math

import jax
import jax.numpy as jnp
from jax import lax
from jax.experimental import pallas as pl
from jax.experimental.pallas import tpu as pltpu

F32 = jnp.float32
BF16 = jnp.bfloat16

RMS_EPS = 1e-6
ROPE_THETA = 10000.0
HEAD_DIM = 64
GROUP_WIDTH = 512
CONV_TAPS = 4
POOL_WINDOWS = (2, 4, 8, 16)
POOL_BUF = 15
SSD_CHUNK = 128
SSD_HEADS = 8
SSD_GROUPS = 2
ATTN_HEADS = 4
PAGE_SIZE = 128
NEG_BIG = -1e30

VMEM_LIMIT_BYTES = 56 * 1024 * 1024

OFF_Z, OFF_XS, OFF_POOL, OFF_Q, OFF_K, OFF_V, OFF_BC, OFF_DT = 0, 512, 1024, 1536, 2048, 2560, 3072, 3328
IN_PROJ_PADDED = 3456


def _params(*semantics):
    return pltpu.CompilerParams(dimension_semantics=semantics, vmem_limit_bytes=VMEM_LIMIT_BYTES)


def _silu(x):
    return x / (1.0 + jnp.exp(-x))


def _dot(a, b):
    return jnp.dot(a, b, preferred_element_type=F32)


def _dot_nt(a, b):
    return lax.dot_general(a, b, (((1,), (1,)), ((), ())), preferred_element_type=F32)


def _dot_tn(a, b):
    return lax.dot_general(a, b, (((0,), (0,)), ((), ())), preferred_element_type=F32)


def _split3(x):
    hi = x.astype(BF16)
    r1 = x - hi.astype(F32)
    mid = r1.astype(BF16)
    lo = (r1 - mid.astype(F32)).astype(BF16)
    return hi, mid, lo


def _in_proj_kernel(x_ref, g_ref, w_ref, cos_ref, sin_ref, qg_ref, kg_ref, seg_ref,
                    z_ref, xs_ref, pool_ref, q_ref, k_ref, v_ref, bc_ref, dt_ref):
    x = x_ref[...]
    ms = jnp.mean(x * x, axis=-1, keepdims=True)
    u = (x * lax.rsqrt(ms + RMS_EPS) * g_ref[...]).astype(BF16)

    def proj(off, width):
        return _dot(u, w_ref[:, off:off + width])

    z_ref[...] = proj(OFF_Z, GROUP_WIDTH)
    xs_ref[...] = proj(OFF_XS, GROUP_WIDTH)
    pool_ref[...] = proj(OFF_POOL, GROUP_WIDTH)
    v_ref[...] = proj(OFF_V, GROUP_WIDTH)
    bc_ref[...] = proj(OFF_BC, 256)
    dt_ref[...] = proj(OFF_DT, 128)

    cos = cos_ref[...]
    sin = sin_ref[...]
    seg = seg_ref[...]
    lane = lax.broadcasted_iota(jnp.int32, (1, 128), 1)
    first_half = (lane % HEAD_DIM) < (HEAD_DIM // 2)

    def norm_rope(off, gain_ref, out_ref):
        gain = gain_ref[...]
        for s in range(GROUP_WIDTH // 128):
            y = proj(off + s * 128, 128)
            y2 = y * y
            hi = y2.astype(BF16)
            lo = (y2 - hi.astype(F32)).astype(BF16)
            seg_ms = _dot(hi, seg) + _dot(lo, seg)
            yn = y * lax.rsqrt(seg_ms + RMS_EPS) * gain
            rot = jnp.where(first_half, -pltpu.roll(yn, 128 - HEAD_DIM // 2, 1),
                            pltpu.roll(yn, HEAD_DIM // 2, 1))
            out_ref[:, s * 128:(s + 1) * 128] = yn * cos + rot * sin

    norm_rope(OFF_Q, qg_ref, q_ref)
    norm_rope(OFF_K, kg_ref, k_ref)


def _in_proj(layer, x, norm_mix, w_in_p, cos_t, sin_t, q_gain, k_gain, seg_mat, tm):
    n_tok = x.shape[0]
    n_pos_blocks = cos_t.shape[0] // tm
    row = lambda i: (i, 0)
    widths = (GROUP_WIDTH,) * 6 + (256, 128)
    return pl.pallas_call(
        _in_proj_kernel,
        grid=(n_tok // tm,),
        in_specs=[
            pl.BlockSpec((tm, x.shape[1]), row),
            pl.BlockSpec((None, 1, x.shape[1]), lambda i: (layer, 0, 0)),
            pl.BlockSpec((None, x.shape[1], IN_PROJ_PADDED), lambda i: (layer, 0, 0)),
            pl.BlockSpec((tm, 128), lambda i: (i % n_pos_blocks, 0)),
            pl.BlockSpec((tm, 128), lambda i: (i % n_pos_blocks, 0)),
            pl.BlockSpec((None, 1, 128), lambda i: (layer, 0, 0)),
            pl.BlockSpec((None, 1, 128), lambda i: (layer, 0, 0)),
            pl.BlockSpec((128, 128), lambda i: (0, 0)),
        ],
        out_specs=[pl.BlockSpec((tm, w), row) for w in widths],
        out_shape=[jax.ShapeDtypeStruct((n_tok, w), F32) for w in widths],
        compiler_params=_params("parallel"),
        name="in_proj",
    )(x, norm_mix, w_in_p, cos_t, sin_t, q_gain, k_gain, seg_mat)


def _ssd_kernel(z_ref, xs_ref, bc_ref, dt_ref, cbx_ref, cbbc_ref, h0_ref,
                cwx_ref, cwbc_ref, cbiasx_ref, cbiasbc_ref, dtb_ref, alog_ref, dskip_ref, gain_ref,
                y_ref, ncx_ref, ncbc_ref, hout_ref,
                extx_ref, extbc_ref, h_ref, *, valid):
    Q = SSD_CHUNK
    c = pl.program_id(1)

    @pl.when(c == 0)
    def _():
        extx_ref[0:8, :] = cbx_ref[...]
        extbc_ref[0:8, :] = cbbc_ref[...]
        h_ref[...] = h0_ref[...]
        if valid < Q:
            extx_ref[8 + valid:8 + Q, :] = jnp.zeros((Q - valid, GROUP_WIDTH), F32)
            extbc_ref[8 + valid:8 + Q, :] = jnp.zeros((Q - valid, 256), F32)

    extx_ref[8:8 + valid, :] = xs_ref[...]
    extbc_ref[8:8 + valid, :] = bc_ref[...]

    def conv(ext_ref, w_ref, b_ref):
        acc = b_ref[...] + ext_ref[8:8 + Q, :] * w_ref[CONV_TAPS - 1:CONV_TAPS, :]
        for i in range(CONV_TAPS - 1):
            lo = 8 - (CONV_TAPS - 1) + i
            acc = acc + ext_ref[lo:lo + Q, :] * w_ref[i:i + 1, :]
        return _silu(acc)

    xs = conv(extx_ref, cwx_ref, cbiasx_ref)
    bc = conv(extbc_ref, cwbc_ref, cbiasbc_ref)
    tailx = extx_ref[valid:valid + 8, :]
    tailbc = extbc_ref[valid:valid + 8, :]
    extx_ref[0:8, :] = tailx
    extbc_ref[0:8, :] = tailbc
    ncx_ref[...] = tailx
    ncbc_ref[...] = tailbc

    dtr = dt_ref[...] + dtb_ref[...]
    dt = jnp.maximum(dtr, 0.0) + jnp.log1p(jnp.exp(-jnp.abs(dtr)))
    if valid < Q:
        dt = jnp.concatenate([dt, jnp.zeros((Q - valid, 128), F32)], axis=0)
    a = -jnp.exp(alog_ref[...])
    d_a = dt * a
    ri = lax.broadcasted_iota(jnp.int32, (Q, Q), 0)
    ci = lax.broadcasted_iota(jnp.int32, (Q, Q), 1)
    causal = ri >= ci
    tri = causal.astype(BF16)
    hi, mid, lo = _split3(d_a)
    cum = _dot(tri, hi) + _dot(tri, mid) + _dot(tri, lo)
    cum_t = cum.T
    dt_t = dt.T
    cum_last = cum[Q - 1:Q, :]
    e_cum = jnp.exp(cum)
    to_end = jnp.exp(cum_last - cum) * dt
    e_last = jnp.exp(cum_last)

    xs_b = xs.astype(BF16)
    h_prev = h_ref[...]
    h_prev_b = h_prev.astype(BF16)
    heads_per_group = SSD_HEADS // SSD_GROUPS
    gw = heads_per_group * HEAD_DIM
    y_parts = []
    xw_parts = []
    dec_parts = []
    for g in range(SSD_GROUPS):
        b_g = bc[:, g * HEAD_DIM:(g + 1) * HEAD_DIM].astype(BF16)
        c_g = bc[:, 128 + g * HEAD_DIM:128 + (g + 1) * HEAD_DIM].astype(BF16)
        cb = _dot_nt(c_g, b_g)
        y_off_g = _dot(c_g, h_prev_b[:, g * gw:(g + 1) * gw])
        for hh in range(heads_per_group):
            h = g * heads_per_group + hh
            seg = cum[:, h:h + 1] - cum_t[h:h + 1, :]
            decay = jnp.exp(jnp.where(causal, seg, NEG_BIG))
            scores = (cb * decay * dt_t[h:h + 1, :]).astype(BF16)
            x_h = xs_b[:, h * HEAD_DIM:(h + 1) * HEAD_DIM]
            y_h = _dot(scores, x_h)
            y_h = y_h + y_off_g[:, hh * HEAD_DIM:(hh + 1) * HEAD_DIM] * e_cum[:, h:h + 1]
            y_h = y_h + dskip_ref[:, h * HEAD_DIM:(h + 1) * HEAD_DIM] * xs[:, h * HEAD_DIM:(h + 1) * HEAD_DIM]
            y_parts.append(y_h)
            xw_parts.append(xs[:, h * HEAD_DIM:(h + 1) * HEAD_DIM] * to_end[:, h:h + 1])
            dec_parts.append(jnp.broadcast_to(e_last[:, h:h + 1], (1, HEAD_DIM)))
        xw_g = jnp.concatenate(xw_parts[-heads_per_group:], axis=1).astype(BF16)
        st_g = _dot_tn(b_g, xw_g)
        dec_g = jnp.concatenate(dec_parts[-heads_per_group:], axis=1)
        h_ref[:, g * gw:(g + 1) * gw] = dec_g * h_prev[:, g * gw:(g + 1) * gw] + st_g
    hout_ref[...] = h_ref[...]

    y = jnp.concatenate(y_parts, axis=1)[0:valid, :]
    y = y * _silu(z_ref[...])
    ms = jnp.mean(y * y, axis=-1, keepdims=True)
    y_ref[...] = y * lax.rsqrt(ms + RMS_EPS) * gain_ref[...]


def _ssd(layer, z, xs, bc, dt, cbx, cbbc, h0t, cwx, cwbc, cbiasx, cbiasbc, dtb, alog, dskip, gain,
         n_seq, seq_len):
    valid = min(seq_len, SSD_CHUNK)
    n_chunks = seq_len // valid
    tok = lambda b, c: (b * n_chunks + c, 0)
    seq3 = lambda b, c: (b, 0, 0)
    lay3 = lambda b, c: (layer, 0, 0)
    n_tok = n_seq * seq_len
    kern = functools.partial(_ssd_kernel, valid=valid)
    return pl.pallas_call(
        kern,
        grid=(n_seq, n_chunks),
        in_specs=[
            pl.BlockSpec((valid, GROUP_WIDTH), tok),
            pl.BlockSpec((valid, GROUP_WIDTH), tok),
            pl.BlockSpec((valid, 256), tok),
            pl.BlockSpec((valid, 128), tok),
            pl.BlockSpec((None, 8, GROUP_WIDTH), seq3),
            pl.BlockSpec((None, 8, 256), seq3),
            pl.BlockSpec((None, HEAD_DIM, GROUP_WIDTH), seq3),
            pl.BlockSpec((None, 8, GROUP_WIDTH), lay3),
            pl.BlockSpec((None, 8, 256), lay3),
            pl.BlockSpec((None, 1, GROUP_WIDTH), lay3),
            pl.BlockSpec((None, 1, 256), lay3),
            pl.BlockSpec((None, 1, 128), lay3),
            pl.BlockSpec((None, 1, 128), lay3),
            pl.BlockSpec((None, 1, GROUP_WIDTH), lay3),
            pl.BlockSpec((None, 1, GROUP_WIDTH), lay3),
        ],
        out_specs=[
            pl.BlockSpec((valid, GROUP_WIDTH), tok),
            pl.BlockSpec((None, 8, GROUP_WIDTH), seq3),
            pl.BlockSpec((None, 8, 256), seq3),
            pl.BlockSpec((None, HEAD_DIM, GROUP_WIDTH), seq3),
        ],
        out_shape=[
            jax.ShapeDtypeStruct((n_tok, GROUP_WIDTH), F32),
            jax.ShapeDtypeStruct((n_seq, 8, GROUP_WIDTH), F32),
            jax.ShapeDtypeStruct((n_seq, 8, 256), F32),
            jax.ShapeDtypeStruct((n_seq, HEAD_DIM, GROUP_WIDTH), F32),
        ],
        scratch_shapes=[
            pltpu.VMEM((8 + SSD_CHUNK, GROUP_WIDTH), F32),
            pltpu.VMEM((8 + SSD_CHUNK, 256), F32),
            pltpu.VMEM((HEAD_DIM, GROUP_WIDTH), F32),
        ],
        compiler_params=_params("parallel", "arbitrary"),
        name="ssd",
    )(z, xs, bc, dt, cbx, cbbc, h0t, cwx, cwbc, cbiasx, cbiasbc, dtb, alog, dskip, gain)


def _pool_kernel(x_ref, buf_ref, w_ref, scale_ref, y_ref, nbuf_ref, ext_ref, *, tl, pos0):
    c = pl.program_id(1)

    @pl.when(c == 0)
    def _():
        ext_ref[0:16, :] = buf_ref[...]

    ext_ref[16:16 + tl, :] = x_ref[...]
    row = lax.broadcasted_iota(jnp.int32, (tl, 1), 0)
    pos = pos0 + c * tl + row
    for g, w in enumerate(POOL_WINDOWS):
        sl = slice(g * 128, (g + 1) * 128)
        cur = ext_ref[16:16 + tl, sl]
        win = cur
        for k in range(1, w):
            win = win + ext_ref[16 - k:16 - k + tl, sl]
        cnt = jnp.minimum(w, pos + 1).astype(F32)
        pooled = (win / cnt - cur).astype(BF16)
        y_ref[:, sl] = _dot(pooled, w_ref[g]) * scale_ref[:, sl]
    tail = ext_ref[tl:tl + 16, :]
    ext_ref[0:16, :] = tail
    nbuf_ref[...] = tail


def _pool(layer, x, buf, pool_w, pool_scale, n_seq, seq_len, pos0):
    tl = min(seq_len, 256)
    n_chunks = seq_len // tl
    tok = lambda b, c: (b * n_chunks + c, 0)
    kern = functools.partial(_pool_kernel, tl=tl, pos0=pos0)
    return pl.pallas_call(
        kern,
        grid=(n_seq, n_chunks),
        in_specs=[
            pl.BlockSpec((tl, GROUP_WIDTH), tok),
            pl.BlockSpec((None, 16, GROUP_WIDTH), lambda b, c: (b, 0, 0)),
            pl.BlockSpec((None, 4, 128, 128), lambda b, c: (layer, 0, 0, 0)),
            pl.BlockSpec((None, 1, GROUP_WIDTH), lambda b, c: (layer, 0, 0)),
        ],
        out_specs=[
            pl.BlockSpec((tl, GROUP_WIDTH), tok),
            pl.BlockSpec((None, 16, GROUP_WIDTH), lambda b, c: (b, 0, 0)),
        ],
        out_shape=[
            jax.ShapeDtypeStruct((n_seq * seq_len, GROUP_WIDTH), F32),
            jax.ShapeDtypeStruct((n_seq, 16, GROUP_WIDTH), F32),
        ],
        scratch_shapes=[pltpu.VMEM((16 + tl, GROUP_WIDTH), F32)],
        compiler_params=_params("parallel", "arbitrary"),
        name="pool",
    )(x, buf, pool_w, pool_scale)


def _lam_value(lam_ref):
    lp = lam_ref[...]
    s1 = jnp.sum(lp[0:1, :] * lp[1:2, :], axis=-1, keepdims=True)
    s2 = jnp.sum(lp[2:3, :] * lp[3:4, :], axis=-1, keepdims=True)
    lam_init = lp[4:5, 0:1]
    return jnp.exp(s1) - jnp.exp(s2) + lam_init, lam_init


def _sub_norm(o, gain, lam_init):
    ms = jnp.mean(o * o, axis=-1, keepdims=True)
    return o * lax.rsqrt(ms + RMS_EPS) * gain * (1.0 - lam_init)


def _prompt_attn_kernel(q_ref, k_ref, v_ref, lam_ref, gain_ref, o_ref, m_ref, l_ref, acc_ref, *, tq):
    i = pl.program_id(1)
    j = pl.program_id(2)

    @pl.when(j == 0)
    def _():
        m_ref[...] = jnp.full(m_ref.shape, NEG_BIG, F32)
        l_ref[...] = jnp.zeros(l_ref.shape, F32)
        acc_ref[...] = jnp.zeros(acc_ref.shape, F32)

    def step(masked):
        if masked:
            ri = lax.broadcasted_iota(jnp.int32, (tq, tq), 0)
            ci = lax.broadcasted_iota(jnp.int32, (tq, tq), 1)
            keep = ci <= ri
        for h in range(ATTN_HEADS):
            v_h = v_ref[:, h * 128:(h + 1) * 128].astype(BF16)
            for c in range(2):
                hc = 2 * h + c
                sl = slice(hc * HEAD_DIM, (hc + 1) * HEAD_DIM)
                q_hc = (q_ref[:, sl] * (HEAD_DIM ** -0.5)).astype(BF16)
                k_hc = k_ref[:, sl].astype(BF16)
                s = _dot_nt(q_hc, k_hc)
                if masked:
                    s = jnp.where(keep, s, NEG_BIG)
                m_prev = m_ref[hc]
                m_new = jnp.maximum(m_prev, jnp.max(s, axis=-1, keepdims=True))
                alpha = jnp.exp(m_prev - m_new)
                p = jnp.exp(s - m_new)
                l_ref[hc] = alpha * l_ref[hc] + jnp.sum(p, axis=-1, keepdims=True)
                acc_ref[hc] = alpha * acc_ref[hc] + _dot(p.astype(BF16), v_h)
                m_ref[hc] = m_new

    @pl.when(j < i)
    def _():
        step(False)

    @pl.when(j == i)
    def _():
        step(True)
        lam, lam_init = _lam_value(lam_ref)
        gain = gain_ref[...]
        for h in range(ATTN_HEADS):
            o = acc_ref[2 * h] / l_ref[2 * h] - lam * (acc_ref[2 * h + 1] / l_ref[2 * h + 1])
            o_ref[:, h * 128:(h + 1) * 128] = _sub_norm(o, gain, lam_init)


def _prompt_attn(layer, q, k, v, lam_p, subln, n_seq, seq_len):
    tq = min(seq_len, 256)
    nq = seq_len // tq
    kv_map = lambda b, i, j: (b * nq + jnp.minimum(j, i), 0)
    kern = functools.partial(_prompt_attn_kernel, tq=tq)
    return pl.pallas_call(
        kern,
        grid=(n_seq, nq, nq),
        in_specs=[
            pl.BlockSpec((tq, GROUP_WIDTH), lambda b, i, j: (b * nq + i, 0)),
            pl.BlockSpec((tq, GROUP_WIDTH), kv_map),
            pl.BlockSpec((tq, GROUP_WIDTH), kv_map),
            pl.BlockSpec((None, 8, 128), lambda b, i, j: (layer, 0, 0)),
            pl.BlockSpec((None, 1, 128), lambda b, i, j: (layer, 0, 0)),
        ],
        out_specs=pl.BlockSpec((tq, GROUP_WIDTH), lambda b, i, j: (b * nq + i, 0)),
        out_shape=jax.ShapeDtypeStruct((n_seq * seq_len, GROUP_WIDTH), F32),
        scratch_shapes=[
            pltpu.VMEM((2 * ATTN_HEADS, tq, 1), F32),
            pltpu.VMEM((2 * ATTN_HEADS, tq, 1), F32),
            pltpu.VMEM((2 * ATTN_HEADS, tq, 128), F32),
        ],
        compiler_params=_params("parallel", "parallel", "arbitrary"),
        name="prompt_attn",
    )(q, k, v, lam_p, subln)


PAGES_PER_STEP = 8


def _decode_attn_kernel(pt_ref, qrows_ref, *refs, n_new, n_steps):
    P = PAGES_PER_STEP
    k_refs = refs[0:P]
    v_refs = refs[P:2 * P]
    knew_ref, vnew_ref, lam_ref, gain_ref, o_ref, m_ref, l_ref, acc_ref = refs[2 * P:]
    j = pl.program_id(1)
    n_rows = 2 * ATTN_HEADS * n_new

    @pl.when(j == 0)
    def _():
        m_ref[...] = jnp.full(m_ref.shape, NEG_BIG, F32)
        l_ref[...] = jnp.zeros(l_ref.shape, F32)
        acc_ref[...] = jnp.zeros(acc_ref.shape, F32)

    qrows = qrows_ref[...]

    def update(s_list, v_list):
        m_prev = m_ref[...]
        m_new = m_prev
        for s in s_list:
            m_new = jnp.maximum(m_new, jnp.max(s, axis=-1, keepdims=True))
        alpha = jnp.exp(m_prev - m_new)
        l_new = alpha * l_ref[...]
        pv = [jnp.zeros((2 * n_new, 128), F32) for _ in range(ATTN_HEADS)]
        for s, v in zip(s_list, v_list):
            p = jnp.exp(s - m_new)
            l_new = l_new + jnp.sum(p, axis=-1, keepdims=True)
            p_b = p.astype(BF16)
            for h in range(ATTN_HEADS):
                rows = slice(h * 2 * n_new, (h + 1) * 2 * n_new)
                pv[h] = pv[h] + _dot(p_b[rows, :], v[:, h * 128:(h + 1) * 128])
        for h in range(ATTN_HEADS):
            rows = slice(h * 2 * n_new, (h + 1) * 2 * n_new)
            acc_ref[rows, :] = alpha[rows, :] * acc_ref[rows, :] + pv[h]
        l_ref[...] = l_new
        m_ref[...] = m_new

    @pl.when(j < n_steps)
    def _():
        s_list = []
        v_list = []
        for p in range(P):
            s_list.append(_dot_nt(qrows, k_refs[p][...].astype(BF16)))
            v_list.append(v_refs[p][...].astype(BF16))
        update(s_list, v_list)

    @pl.when(j == n_steps)
    def _():
        s = _dot_nt(qrows, knew_ref[...].astype(BF16))
        key = lax.broadcasted_iota(jnp.int32, (n_rows, 128), 1)
        tok = lax.broadcasted_iota(jnp.int32, (n_rows, 128), 0) % n_new
        s = jnp.where(key <= tok, s, NEG_BIG)
        update([s], [vnew_ref[...].astype(BF16)])
        lam, lam_init = _lam_value(lam_ref)
        gain = gain_ref[...]
        for h in range(ATTN_HEADS):
            r1 = slice(h * 2 * n_new, h * 2 * n_new + n_new)
            r2 = slice(h * 2 * n_new + n_new, (h + 1) * 2 * n_new)
            o = acc_ref[r1, :] / l_ref[r1, :] - lam * (acc_ref[r2, :] / l_ref[r2, :])
            o_ref[:, h * 128:(h + 1) * 128] = _sub_norm(o, gain, lam_init)


def _decode_attn(layer, page_table_flat, qrows, cache_k, cache_v, knew, vnew, lam_p, subln,
                 n_seq, n_new, n_pages):
    P = PAGES_PER_STEP
    n_steps = n_pages // P
    n_rows = 2 * ATTN_HEADS * n_new

    def page_spec(p):
        def index(b, j, pt):
            step = jnp.minimum(j, n_steps - 1)
            return (layer, pt[b * n_pages + step * P + p], 0, 0)
        return pl.BlockSpec((None, None, PAGE_SIZE, GROUP_WIDTH), index)

    seq3 = lambda b, j, pt: (b, 0, 0)
    kern = functools.partial(_decode_attn_kernel, n_new=n_new, n_steps=n_steps)
    grid_spec = pltpu.PrefetchScalarGridSpec(
        num_scalar_prefetch=1,
        grid=(n_seq, n_steps + 1),
        in_specs=(
            [pl.BlockSpec((None, n_rows, GROUP_WIDTH), seq3)]
            + [page_spec(p) for p in range(P)]
            + [page_spec(p) for p in range(P)]
            + [pl.BlockSpec((None, PAGE_SIZE, GROUP_WIDTH), seq3),
               pl.BlockSpec((None, PAGE_SIZE, GROUP_WIDTH), seq3),
               pl.BlockSpec((None, 8, 128), lambda b, j, pt: (layer, 0, 0)),
               pl.BlockSpec((None, 1, 128), lambda b, j, pt: (layer, 0, 0))]
        ),
        out_specs=pl.BlockSpec((n_new, GROUP_WIDTH), lambda b, j, pt: (b, 0)),
        scratch_shapes=[
            pltpu.VMEM((n_rows, 1), F32),
            pltpu.VMEM((n_rows, 1), F32),
            pltpu.VMEM((n_rows, 128), F32),
        ],
    )
    return pl.pallas_call(
        kern,
        grid_spec=grid_spec,
        out_shape=jax.ShapeDtypeStruct((n_seq * n_new, GROUP_WIDTH), F32),
        compiler_params=_params("parallel", "arbitrary"),
        name="decode_attn",
    )(page_table_flat, qrows, *([cache_k] * P), *([cache_v] * P), knew, vnew, lam_p, subln)


def _out_ffn_kernel(h_ref, ys_ref, yp_ref, ya_ref, wo_ref, g_ref, wg_ref, wu_ref, wd_ref,
                    o_ref, acc_ref, u_ref):
    f = pl.program_id(1)

    @pl.when(f == 0)
    def _():
        h1 = h_ref[...]
        h1 = h1 + _dot(ys_ref[...].astype(BF16), wo_ref[0:GROUP_WIDTH, :])
        h1 = h1 + _dot(yp_ref[...].astype(BF16), wo_ref[GROUP_WIDTH:2 * GROUP_WIDTH, :])
        h1 = h1 + _dot(ya_ref[...].astype(BF16), wo_ref[2 * GROUP_WIDTH:3 * GROUP_WIDTH, :])
        acc_ref[...] = h1
        ms = jnp.mean(h1 * h1, axis=-1, keepdims=True)
        u_ref[...] = (h1 * lax.rsqrt(ms + RMS_EPS) * g_ref[...]).astype(BF16)

    u = u_ref[...]
    act = (_silu(_dot(u, wg_ref[...])) * _dot(u, wu_ref[...])).astype(BF16)
    acc_ref[...] += _dot(act, wd_ref[...])

    @pl.when(f == pl.num_programs(1) - 1)
    def _():
        o_ref[...] = acc_ref[...]


def _out_ffn(layer, h, ys, yp, ya, w_out_b, norm_ffn, w_gu_b, w_down_b, tm, tf):
    n_tok, d = h.shape
    ffn = w_down_b.shape[1]
    nf = ffn // tf
    row = lambda i, f: (i, 0)
    return pl.pallas_call(
        _out_ffn_kernel,
        grid=(n_tok // tm, nf),
        in_specs=[
            pl.BlockSpec((tm, d), row),
            pl.BlockSpec((tm, GROUP_WIDTH), row),
            pl.BlockSpec((tm, GROUP_WIDTH), row),
            pl.BlockSpec((tm, GROUP_WIDTH), row),
            pl.BlockSpec((None, 3 * GROUP_WIDTH, d), lambda i, f: (layer, 0, 0)),
            pl.BlockSpec((None, 1, d), lambda i, f: (layer, 0, 0)),
            pl.BlockSpec((None, d, tf), lambda i, f: (layer, 0, f)),
            pl.BlockSpec((None, d, tf), lambda i, f: (layer, 0, nf + f)),
            pl.BlockSpec((None, tf, d), lambda i, f: (layer, f, 0)),
        ],
        out_specs=pl.BlockSpec((tm, d), row),
        out_shape=jax.ShapeDtypeStruct((n_tok, d), F32),
        scratch_shapes=[pltpu.VMEM((tm, d), F32), pltpu.VMEM((tm, d), BF16)],
        compiler_params=_params("parallel", "arbitrary"),
        name="out_ffn",
    )(h, ys, yp, ya, w_out_b, norm_ffn, w_gu_b, w_gu_b, w_down_b)


def _rope_tables(pos):
    half = HEAD_DIM // 2
    inv_freq = ROPE_THETA ** (-jnp.arange(half, dtype=F32) / half)
    ang = pos.astype(F32)[:, None] * inv_freq[None, :]
    return jnp.tile(jnp.cos(ang), (1, 128 // half)), jnp.tile(jnp.sin(ang), (1, 128 // half))


def _pad_rows_front(x, rows):
    pad = jnp.zeros(x.shape[:-2] + (rows - x.shape[-2], x.shape[-1]), x.dtype)
    return jnp.concatenate([pad, x], axis=-2)


def kernel(x_prompt, x_sample, cache_k, cache_v, state_conv, state_ssm, state_pool, page_table, norm_mix, w_in, conv_w, conv_b, dt_bias, a_log, d_skip, ssd_norm, pool_w, pool_scale, q_norm, k_norm, lam_q1, lam_k1, lam_q2, lam_k2, subln, w_out, norm_ffn, w_gate_up, w_down):
    depth = w_in.shape[0]
    bp, lp, d = x_prompt.shape
    bs, ls, _ = x_sample.shape
    n_pages = page_table.shape[1]
    past_len = n_pages * PAGE_SIZE
    ffn = w_down.shape[1]

    c0 = GROUP_WIDTH
    c1 = c0 + GROUP_WIDTH
    c2 = c1 + 256
    c3 = c2 + SSD_HEADS
    w_in_p = jnp.concatenate(
        [w_in[..., 0:c1], w_in[..., c3:], w_in[..., c1:c2], w_in[..., c2:c3],
         jnp.zeros(w_in.shape[:2] + (128 - SSD_HEADS,), w_in.dtype)], axis=-1).astype(BF16)
    w_out_b = w_out.astype(BF16)
    w_gu_b = w_gate_up.astype(BF16)
    w_down_b = w_down.astype(BF16)
    pool_w_b = pool_w.astype(BF16)
    norm_mix3 = norm_mix[:, None, :]
    norm_ffn3 = norm_ffn[:, None, :]
    q_gain = jnp.tile(q_norm, (1, 2))[:, None, :]
    k_gain = jnp.tile(k_norm, (1, 2))[:, None, :]
    blk = jnp.arange(128) // HEAD_DIM
    seg_mat = ((blk[:, None] == blk[None, :]).astype(F32) / HEAD_DIM).astype(BF16)
    cw = jnp.concatenate([conv_w, jnp.zeros((depth, 8 - CONV_TAPS, conv_w.shape[-1]), F32)], axis=1)
    cwx, cwbc = cw[..., :GROUP_WIDTH], cw[..., GROUP_WIDTH:]
    cbiasx = conv_b[:, None, :GROUP_WIDTH]
    cbiasbc = conv_b[:, None, GROUP_WIDTH:]
    pad_heads = lambda x: jnp.concatenate([x, jnp.zeros((depth, 128 - SSD_HEADS), x.dtype)], axis=-1)[:, None, :]
    dtb = pad_heads(dt_bias)
    alog = pad_heads(a_log)
    dskip = jnp.repeat(d_skip, HEAD_DIM, axis=-1)[:, None, :]
    ssd_gain = ssd_norm[:, None, :]
    pool_scale3 = pool_scale[:, None, :]
    subln3 = subln[:, None, :]
    lam_inits = jnp.asarray([0.8 - 0.6 * math.exp(-0.3 * l) for l in range(depth)], F32)
    lam_rows = jnp.stack([lam_q1, lam_k1, lam_q2, lam_k2], axis=1)
    lam_rows = jnp.concatenate([lam_rows, jnp.zeros((depth, 4, 128 - HEAD_DIM), F32)], axis=-1)
    lam_p = jnp.concatenate(
        [lam_rows, jnp.broadcast_to(lam_inits[:, None, None], (depth, 1, 128)),
         jnp.zeros((depth, 3, 128), F32)], axis=1)

    cos_p, sin_p = _rope_tables(jnp.arange(lp, dtype=jnp.int32))
    pos_s = past_len + jnp.arange(ls, dtype=jnp.int32)
    cos_s, sin_s = _rope_tables(jnp.tile(pos_s, bs))

    zc = jnp.zeros((bp, 8, GROUP_WIDTH + 256), F32)
    cb_s = _pad_rows_front(state_conv, 8)
    h0_p = jnp.zeros((bp, HEAD_DIM, GROUP_WIDTH), F32)
    h0_s = jnp.transpose(state_ssm, (0, 1, 4, 2, 3)).reshape(depth, bs, HEAD_DIM, GROUP_WIDTH)
    pb_p = jnp.zeros((bp, 16, GROUP_WIDTH), F32)
    pb_s = _pad_rows_front(state_pool, 16)
    page_flat = page_table.reshape(-1).astype(jnp.int32)
    n_pool = cache_k.shape[1]
    cache_k2 = cache_k.reshape(depth, n_pool, PAGE_SIZE, GROUP_WIDTH)
    cache_v2 = cache_v.reshape(depth, n_pool, PAGE_SIZE, GROUP_WIDTH)
    eye_hc = jnp.eye(2 * ATTN_HEADS, dtype=F32)

    hp = x_prompt.reshape(bp * lp, d)
    hs = x_sample.reshape(bs * ls, d)
    tm_p = 512 if (bp * lp) % 512 == 0 and lp % 512 == 0 else lp
    tm_s = bs * ls
    tf = ffn // 4 if (ffn // 4) % 128 == 0 else ffn

    outs_p = [[], [], [], [], []]
    outs_s = [[], [], [], [], []]

    def state_outputs(k, v, ncx, ncbc, hout, nbuf, n_seq, seq_len, store):
        store[0].append(k.reshape(n_seq, seq_len, ATTN_HEADS, 2, HEAD_DIM))
        store[1].append(v.reshape(n_seq, seq_len, ATTN_HEADS, 2 * HEAD_DIM))
        store[2].append(jnp.concatenate([ncx, ncbc], axis=-1)[:, 8 - (CONV_TAPS - 1):, :])
        store[3].append(jnp.transpose(hout.reshape(n_seq, HEAD_DIM, SSD_HEADS, HEAD_DIM), (0, 2, 3, 1)))
        store[4].append(nbuf[:, 16 - POOL_BUF:, :])

    for l in range(depth):
        ssd_w = (cwx, cwbc, cbiasx, cbiasbc, dtb, alog, dskip, ssd_gain)
        z, xs, pin, q, k, v, bc, dt = _in_proj(l, hp, norm_mix3, w_in_p, cos_p, sin_p, q_gain, k_gain, seg_mat, tm_p)
        y_ssd, ncx, ncbc, hout = _ssd(l, z, xs, bc, dt, zc[..., :GROUP_WIDTH], zc[..., GROUP_WIDTH:], h0_p,
                                      *ssd_w, n_seq=bp, seq_len=lp)
        y_pool, nbuf = _pool(l, pin, pb_p, pool_w_b, pool_scale3, bp, lp, 0)
        y_attn = _prompt_attn(l, q, k, v, lam_p, subln3, bp, lp)
        hp = _out_ffn(l, hp, y_ssd, y_pool, y_attn, w_out_b, norm_ffn3, w_gu_b, w_down_b, tm_p, tf)
        state_outputs(k, v, ncx, ncbc, hout, nbuf, bp, lp, outs_p)

        z, xs, pin, q, k, v, bc, dt = _in_proj(l, hs, norm_mix3, w_in_p, cos_s, sin_s, q_gain, k_gain, seg_mat, tm_s)
        y_ssd, ncx, ncbc, hout = _ssd(l, z, xs, bc, dt, cb_s[l, ..., :GROUP_WIDTH], cb_s[l, ..., GROUP_WIDTH:],
                                      h0_s[l], *ssd_w, n_seq=bs, seq_len=ls)
        y_pool, nbuf = _pool(l, pin, pb_s[l], pool_w_b, pool_scale3, bs, ls, past_len)
        q4 = (q * (HEAD_DIM ** -0.5)).reshape(bs, ls, 2 * ATTN_HEADS, HEAD_DIM)
        qrows = jnp.transpose(q4[:, :, :, None, :] * eye_hc[None, None, :, :, None], (0, 2, 1, 3, 4))
        qrows = qrows.reshape(bs, 2 * ATTN_HEADS * ls, GROUP_WIDTH).astype(BF16)
        pad_new = lambda x: jnp.concatenate(
            [x.reshape(bs, ls, GROUP_WIDTH), jnp.zeros((bs, PAGE_SIZE - ls, GROUP_WIDTH), F32)], axis=1)
        y_attn = _decode_attn(l, page_flat, qrows, cache_k2, cache_v2, pad_new(k), pad_new(v), lam_p, subln3,
                              bs, ls, n_pages)
        hs = _out_ffn(l, hs, y_ssd, y_pool, y_attn, w_out_b, norm_ffn3, w_gu_b, w_down_b, tm_s, tf)
        state_outputs(k, v, ncx, ncbc, hout, nbuf, bs, ls, outs_s)

    stacked_p = [jnp.stack(s, axis=0) for s in outs_p]
    stacked_s = [jnp.stack(s, axis=0) for s in outs_s]
    return (hp.reshape(bp, lp, d), hs.reshape(bs, ls, d), *stacked_p, *stacked_s)
```

```python
import functools
import math

import jax
import jax.numpy as jnp
from jax import lax
from jax.experimental import pallas as pl
from jax.experimental.pallas import tpu as pltpu

F32 = jnp.float32
BF16 = jnp.bfloat16

RMS_EPS = 1e-6
ROPE_THETA = 10000.0
HEAD_DIM = 64
GROUP_WIDTH = 512
CONV_TAPS = 4
POOL_WINDOWS = (2, 4, 8, 16)
POOL_BUF = 15
SSD_CHUNK = 128
SSD_HEADS = 8
SSD_GROUPS = 2
ATTN_HEADS = 4
PAGE_SIZE = 128
NEG_BIG = -1e30
QK_SCALE_LOG2 = (HEAD_DIM ** -0.5) * math.log2(math.e)

VMEM_LIMIT_BYTES = 56 * 1024 * 1024

OFF_Z, OFF_XS, OFF_POOL, OFF_Q, OFF_K, OFF_V, OFF_BC, OFF_DT = 0, 512, 1024, 1536, 2048, 2560, 3072, 3328
IN_PROJ_PADDED = 3456


def _params(*semantics):
    return pltpu.CompilerParams(dimension_semantics=semantics, vmem_limit_bytes=VMEM_LIMIT_BYTES)


def _silu(x):
    return x / (1.0 + jnp.exp(-x))


def _dot(a, b):
    return jnp.dot(a, b, preferred_element_type=F32)


def _dot_nt(a, b):
    return lax.dot_general(a, b, (((1,), (1,)), ((), ())), preferred_element_type=F32)


def _dot_tn(a, b):
    return lax.dot_general(a, b, (((0,), (0,)), ((), ())), preferred_element_type=F32)


def _split3(x):
    hi = x.astype(BF16)
    r1 = x - hi.astype(F32)
    mid = r1.astype(BF16)
    lo = (r1 - mid.astype(F32)).astype(BF16)
    return hi, mid, lo


def _in_proj_kernel(x_ref, g_ref, w_ref, cos_ref, sin_ref, qg_ref, kg_ref, seg_ref,
                    z_ref, xs_ref, pool_ref, qb_ref, k_ref, v_ref, kb_ref, vb_ref, bc_ref, dt_ref):
    x = x_ref[...]
    ms = jnp.mean(x * x, axis=-1, keepdims=True)
    u = (x * lax.rsqrt(ms + RMS_EPS) * g_ref[...]).astype(BF16)

    def proj(off, width):
        return _dot(u, w_ref[:, off:off + width])

    z_ref[...] = proj(OFF_Z, GROUP_WIDTH)
    xs_ref[...] = proj(OFF_XS, GROUP_WIDTH)
    pool_ref[...] = proj(OFF_POOL, GROUP_WIDTH)
    v = proj(OFF_V, GROUP_WIDTH)
    v_ref[...] = v
    vb_ref[...] = v.astype(BF16)
    bc_ref[...] = proj(OFF_BC, 256)
    dt_ref[...] = proj(OFF_DT, 128)

    cos = cos_ref[...]
    sin = sin_ref[...]
    seg = seg_ref[...]
    lane = lax.broadcasted_iota(jnp.int32, (1, 128), 1)
    first_half = (lane % HEAD_DIM) < (HEAD_DIM // 2)

    def norm_rope(off, gain_ref, out_ref, out_b_ref, scale):
        gain = gain_ref[...]
        for s in range(GROUP_WIDTH // 256):
            y = proj(off + s * 256, 256)
            seg_ms = _dot((y * y).astype(BF16), seg)
            yn_wide = y * lax.rsqrt(seg_ms + RMS_EPS) * gain
            for t in range(2):
                lo = s * 256 + t * 128
                yn = yn_wide[:, t * 128:(t + 1) * 128]
                rot = jnp.where(first_half, -pltpu.roll(yn, 128 - HEAD_DIM // 2, 1),
                                pltpu.roll(yn, HEAD_DIM // 2, 1))
                roped = yn * cos + rot * sin
                if out_ref is not None:
                    out_ref[:, lo:lo + 128] = roped
                out_b_ref[:, lo:lo + 128] = (roped * scale).astype(BF16)

    norm_rope(OFF_Q, qg_ref, None, qb_ref, QK_SCALE_LOG2)
    norm_rope(OFF_K, kg_ref, k_ref, kb_ref, 1.0)


def _in_proj(layer, x, norm_mix, w_in_p, cos_t, sin_t, q_gain, k_gain, seg_mat, tm):
    n_tok = x.shape[0]
    n_pos_blocks = cos_t.shape[0] // tm
    row = lambda i: (i, 0)
    widths = (GROUP_WIDTH,) * 8 + (256, 128)
    dtypes = (F32, F32, F32, BF16, F32, F32, BF16, BF16, F32, F32)
    return pl.pallas_call(
        _in_proj_kernel,
        grid=(n_tok // tm,),
        in_specs=[
            pl.BlockSpec((tm, x.shape[1]), row),
            pl.BlockSpec((None, 1, x.shape[1]), lambda i: (layer, 0, 0)),
            pl.BlockSpec((None, x.shape[1], IN_PROJ_PADDED), lambda i: (layer, 0, 0)),
            pl.BlockSpec((tm, 128), lambda i: (i % n_pos_blocks, 0)),
            pl.BlockSpec((tm, 128), lambda i: (i % n_pos_blocks, 0)),
            pl.BlockSpec((None, 1, 256), lambda i: (layer, 0, 0)),
            pl.BlockSpec((None, 1, 256), lambda i: (layer, 0, 0)),
            pl.BlockSpec((256, 256), lambda i: (0, 0)),
        ],
        out_specs=[pl.BlockSpec((tm, w), row) for w in widths],
        out_shape=[jax.ShapeDtypeStruct((n_tok, w), dt) for w, dt in zip(widths, dtypes)],
        compiler_params=_params("parallel"),
        name="in_proj",
    )(x, norm_mix, w_in_p, cos_t, sin_t, q_gain, k_gain, seg_mat)


def _ssd_kernel(z_ref, xs_ref, bc_ref, dt_ref, cbx_ref, cbbc_ref, h0_ref,
                cwx_ref, cwbc_ref, cbiasx_ref, cbiasbc_ref, dtb_ref, alog_ref, dskip_ref, gain_ref,
                y_ref, ncx_ref, ncbc_ref, hout_ref,
                extx_ref, extbc_ref, h_ref, *, valid):
    Q = SSD_CHUNK
    c = pl.program_id(1)

    @pl.when(c == 0)
    def _():
        extx_ref[0:8, :] = cbx_ref[...]
        extbc_ref[0:8, :] = cbbc_ref[...]
        h_ref[...] = h0_ref[...]
        if valid < Q:
            extx_ref[8 + valid:8 + Q, :] = jnp.zeros((Q - valid, GROUP_WIDTH), F32)
            extbc_ref[8 + valid:8 + Q, :] = jnp.zeros((Q - valid, 256), F32)

    extx_ref[8:8 + valid, :] = xs_ref[...]
    extbc_ref[8:8 + valid, :] = bc_ref[...]

    def conv(ext_ref, w_ref, b_ref):
        acc = b_ref[...] + ext_ref[8:8 + Q, :] * w_ref[CONV_TAPS - 1:CONV_TAPS, :]
        for i in range(CONV_TAPS - 1):
            lo = 8 - (CONV_TAPS - 1) + i
            acc = acc + ext_ref[lo:lo + Q, :] * w_ref[i:i + 1, :]
        return _silu(acc)

    xs = conv(extx_ref, cwx_ref, cbiasx_ref)
    bc = conv(extbc_ref, cwbc_ref, cbiasbc_ref)
    tailx = extx_ref[valid:valid + 8, :]
    tailbc = extbc_ref[valid:valid + 8, :]
    extx_ref[0:8, :] = tailx
    extbc_ref[0:8, :] = tailbc
    ncx_ref[...] = tailx
    ncbc_ref[...] = tailbc

    dtr = dt_ref[...] + dtb_ref[...]
    dt = jnp.maximum(dtr, 0.0) + jnp.log1p(jnp.exp(-jnp.abs(dtr)))
    if valid < Q:
        dt = jnp.concatenate([dt, jnp.zeros((Q - valid, 128), F32)], axis=0)
    a = -jnp.exp(alog_ref[...])
    d_a = dt * a
    ri = lax.broadcasted_iota(jnp.int32, (Q, Q), 0)
    ci = lax.broadcasted_iota(jnp.int32, (Q, Q), 1)
    causal = ri >= ci
    tri = causal.astype(BF16)
    hi, mid, lo = _split3(d_a)
    cum = _dot(tri, hi) + _dot(tri, mid) + _dot(tri, lo)
    cum_t = cum.T
    dt_t = dt.T
    cum_last = cum[Q - 1:Q, :]
    e_cum = jnp.exp(cum)
    to_end = jnp.exp(cum_last - cum) * dt
    e_last = jnp.exp(cum_last)

    xs_b = xs.astype(BF16)
    h_prev = h_ref[...]
    h_prev_b = h_prev.astype(BF16)
    heads_per_group = SSD_HEADS // SSD_GROUPS
    gw = heads_per_group * HEAD_DIM
    y_parts = []
    xw_parts = []
    dec_parts = []
    for g in range(SSD_GROUPS):
        b_g = bc[:, g * HEAD_DIM:(g + 1) * HEAD_DIM].astype(BF16)
        c_g = bc[:, 128 + g * HEAD_DIM:128 + (g + 1) * HEAD_DIM].astype(BF16)
        cb = _dot_nt(c_g, b_g)
        y_off_g = _dot(c_g, h_prev_b[:, g * gw:(g + 1) * gw])
        for hh in range(heads_per_group):
            h = g * heads_per_group + hh
            seg = cum[:, h:h + 1] - cum_t[h:h + 1, :]
            decay = jnp.exp(jnp.where(causal, seg, NEG_BIG))
            scores = (cb * decay * dt_t[h:h + 1, :]).astype(BF16)
            x_h = xs_b[:, h * HEAD_DIM:(h + 1) * HEAD_DIM]
            y_h = _dot(scores, x_h)
            y_h = y_h + y_off_g[:, hh * HEAD_DIM:(hh + 1) * HEAD_DIM] * e_cum[:, h:h + 1]
            y_h = y_h + dskip_ref[:, h * HEAD_DIM:(h + 1) * HEAD_DIM] * xs[:, h * HEAD_DIM:(h + 1) * HEAD_DIM]
            y_parts.append(y_h)
            xw_parts.append(xs[:, h * HEAD_DIM:(h + 1) * HEAD_DIM] * to_end[:, h:h + 1])
            dec_parts.append(jnp.broadcast_to(e_last[:, h:h + 1], (1, HEAD_DIM)))
        xw_g = jnp.concatenate(xw_parts[-heads_per_group:], axis=1).astype(BF16)
        st_g = _dot_tn(b_g, xw_g)
        dec_g = jnp.concatenate(dec_parts[-heads_per_group:], axis=1)
        h_ref[:, g * gw:(g + 1) * gw] = dec_g * h_prev[:, g * gw:(g + 1) * gw] + st_g
    hout_ref[...] = h_ref[...]

    y = jnp.concatenate(y_parts, axis=1)[0:valid, :]
    y = y * _silu(z_ref[...])
    ms = jnp.mean(y * y, axis=-1, keepdims=True)
    y_ref[...] = y * lax.rsqrt(ms + RMS_EPS) * gain_ref[...]


def _ssd(layer, z, xs, bc, dt, cbx, cbbc, h0t, cwx, cwbc, cbiasx, cbiasbc, dtb, alog, dskip, gain,
         n_seq, seq_len):
    valid = min(seq_len, SSD_CHUNK)
    n_chunks = seq_len // valid
    tok = lambda b, c: (b * n_chunks + c, 0)
    seq3 = lambda b, c: (b, 0, 0)
    lay3 = lambda b, c: (layer, 0, 0)
    n_tok = n_seq * seq_len
    kern = functools.partial(_ssd_kernel, valid=valid)
    return pl.pallas_call(
        kern,
        grid=(n_seq, n_chunks),
        in_specs=[
            pl.BlockSpec((valid, GROUP_WIDTH), tok),
            pl.BlockSpec((valid, GROUP_WIDTH), tok),
            pl.BlockSpec((valid, 256), tok),
            pl.BlockSpec((valid, 128), tok),
            pl.BlockSpec((None, 8, GROUP_WIDTH), seq3),
            pl.BlockSpec((None, 8, 256), seq3),
            pl.BlockSpec((None, HEAD_DIM, GROUP_WIDTH), seq3),
            pl.BlockSpec((None, 8, GROUP_WIDTH), lay3),
            pl.BlockSpec((None, 8, 256), lay3),
            pl.BlockSpec((None, 1, GROUP_WIDTH), lay3),
            pl.BlockSpec((None, 1, 256), lay3),
            pl.BlockSpec((None, 1, 128), lay3),
            pl.BlockSpec((None, 1, 128), lay3),
            pl.BlockSpec((None, 1, GROUP_WIDTH), lay3),
            pl.BlockSpec((None, 1, GROUP_WIDTH), lay3),
        ],
        out_specs=[
            pl.BlockSpec((valid, GROUP_WIDTH), tok),
            pl.BlockSpec((None, 8, GROUP_WIDTH), seq3),
            pl.BlockSpec((None, 8, 256), seq3),
            pl.BlockSpec((None, HEAD_DIM, GROUP_WIDTH), seq3),
        ],
        out_shape=[
            jax.ShapeDtypeStruct((n_tok, GROUP_WIDTH), F32),
            jax.ShapeDtypeStruct((n_seq, 8, GROUP_WIDTH), F32),
            jax.ShapeDtypeStruct((n_seq, 8, 256), F32),
            jax.ShapeDtypeStruct((n_seq, HEAD_DIM, GROUP_WIDTH), F32),
        ],
        scratch_shapes=[
            pltpu.VMEM((8 + SSD_CHUNK, GROUP_WIDTH), F32),
            pltpu.VMEM((8 + SSD_CHUNK, 256), F32),
            pltpu.VMEM((HEAD_DIM, GROUP_WIDTH), F32),
        ],
        compiler_params=_params("parallel", "arbitrary"),
        name="ssd",
    )(z, xs, bc, dt, cbx, cbbc, h0t, cwx, cwbc, cbiasx, cbiasbc, dtb, alog, dskip, gain)


def _pool_kernel(x_ref, buf_ref, w_ref, scale_ref, y_ref, nbuf_ref, ext_ref, *, tl, pos0):
    c = pl.program_id(1)

    @pl.when(c == 0)
    def _():
        ext_ref[0:16, :] = buf_ref[...]

    ext_ref[16:16 + tl, :] = x_ref[...]
    row = lax.broadcasted_iota(jnp.int32, (tl, 1), 0)
    pos = pos0 + c * tl + row
    for g, w in enumerate(POOL_WINDOWS):
        sl = slice(g * 128, (g + 1) * 128)
        cur = ext_ref[16:16 + tl, sl]
        win = cur
        for k in range(1, w):
            win = win + ext_ref[16 - k:16 - k + tl, sl]
        cnt = jnp.minimum(w, pos + 1).astype(F32)
        pooled = (win / cnt - cur).astype(BF16)
        y_ref[:, sl] = _dot(pooled, w_ref[g]) * scale_ref[:, sl]
    tail = ext_ref[tl:tl + 16, :]
    ext_ref[0:16, :] = tail
    nbuf_ref[...] = tail


def _pool(layer, x, buf, pool_w, pool_scale, n_seq, seq_len, pos0):
    tl = min(seq_len, 256)
    n_chunks = seq_len // tl
    tok = lambda b, c: (b * n_chunks + c, 0)
    kern = functools.partial(_pool_kernel, tl=tl, pos0=pos0)
    return pl.pallas_call(
        kern,
        grid=(n_seq, n_chunks),
        in_specs=[
            pl.BlockSpec((tl, GROUP_WIDTH), tok),
            pl.BlockSpec((None, 16, GROUP_WIDTH), lambda b, c: (b, 0, 0)),
            pl.BlockSpec((None, 4, 128, 128), lambda b, c: (layer, 0, 0, 0)),
            pl.BlockSpec((None, 1, GROUP_WIDTH), lambda b, c: (layer, 0, 0)),
        ],
        out_specs=[
            pl.BlockSpec((tl, GROUP_WIDTH), tok),
            pl.BlockSpec((None, 16, GROUP_WIDTH), lambda b, c: (b, 0, 0)),
        ],
        out_shape=[
            jax.ShapeDtypeStruct((n_seq * seq_len, GROUP_WIDTH), F32),
            jax.ShapeDtypeStruct((n_seq, 16, GROUP_WIDTH), F32),
        ],
        scratch_shapes=[pltpu.VMEM((16 + tl, GROUP_WIDTH), F32)],
        compiler_params=_params("parallel", "arbitrary"),
        name="pool",
    )(x, buf, pool_w, pool_scale)


def _lam_value(lam_ref):
    lp = lam_ref[...]
    s1 = jnp.sum(lp[0:1, :] * lp[1:2, :], axis=-1, keepdims=True)
    s2 = jnp.sum(lp[2:3, :] * lp[3:4, :], axis=-1, keepdims=True)
    lam_init = lp[4:5, 0:1]
    return jnp.exp(s1) - jnp.exp(s2) + lam_init, lam_init


def _sub_norm(o, gain, lam_init):
    ms = jnp.mean(o * o, axis=-1, keepdims=True)
    return o * lax.rsqrt(ms + RMS_EPS) * gain * (1.0 - lam_init)


ATTN_ROW_BLOCK = 128


def _prompt_attn_kernel(q_ref, k_ref, v_ref, lam_ref, gain_ref, o_ref, m_ref, acc_ref, *, tq):
    i = pl.program_id(1)
    j = pl.program_id(2)
    rs = min(ATTN_ROW_BLOCK, tq)

    @pl.when(j == 0)
    def _():
        m_ref[...] = jnp.full(m_ref.shape, NEG_BIG, F32)
        acc_ref[...] = jnp.zeros(acc_ref.shape, F32)

    def step(masked):
        ones = jnp.ones((tq, 128), BF16)
        for h in range(ATTN_HEADS):
            v_h = jnp.concatenate([v_ref[:, h * 128:(h + 1) * 128], ones], axis=1)
            for c in range(2):
                hc = 2 * h + c
                sl = slice(hc * HEAD_DIM, (hc + 1) * HEAD_DIM)
                k_hc = k_ref[:, sl]
                for r in range(tq // rs):
                    rows = slice(r * rs, (r + 1) * rs)
                    s = _dot_nt(q_ref[rows, sl], k_hc)
                    if masked:
                        ri = lax.broadcasted_iota(jnp.int32, (rs, tq), 0) + r * rs
                        ci = lax.broadcasted_iota(jnp.int32, (rs, tq), 1)
                        s = jnp.where(ci <= ri, s, NEG_BIG)
                    m_prev = m_ref[hc, rows, :]
                    m_new = jnp.maximum(m_prev, jnp.max(s, axis=-1, keepdims=True))
                    alpha = jnp.exp2(m_prev - m_new)
                    p = jnp.exp2(s - jnp.tile(m_new, (1, tq // 128)))
                    acc_ref[hc, rows, :] = jnp.tile(alpha, (1, 2)) * acc_ref[hc, rows, :] + _dot(p.astype(BF16), v_h)
                    m_ref[hc, rows, :] = m_new

    @pl.when(j < i)
    def _():
        step(False)

    @pl.when(j == i)
    def _():
        step(True)
        lam, lam_init = _lam_value(lam_ref)
        gain = gain_ref[...]
        for h in range(ATTN_HEADS):
            a1 = acc_ref[2 * h]
            a2 = acc_ref[2 * h + 1]
            o = a1[:, :128] / a1[:, 128:] - lam * (a2[:, :128] / a2[:, 128:])
            o_ref[:, h * 128:(h + 1) * 128] = _sub_norm(o, gain, lam_init)


def _prompt_attn(layer, q, k, v, lam_p, subln, n_seq, seq_len):
    tq = min(seq_len, 256)
    nq = seq_len // tq
    kv_map = lambda b, i, j: (b * nq + jnp.minimum(j, i), 0)
    kern = functools.partial(_prompt_attn_kernel, tq=tq)
    return pl.pallas_call(
        kern,
        grid=(n_seq, nq, nq),
        in_specs=[
            pl.BlockSpec((tq, GROUP_WIDTH), lambda b, i, j: (b * nq + i, 0)),
            pl.BlockSpec((tq, GROUP_WIDTH), kv_map),
            pl.BlockSpec((tq, GROUP_WIDTH), kv_map),
            pl.BlockSpec((None, 8, 128), lambda b, i, j: (layer, 0, 0)),
            pl.BlockSpec((None, 1, 128), lambda b, i, j: (layer, 0, 0)),
        ],
        out_specs=pl.BlockSpec((tq, GROUP_WIDTH), lambda b, i, j: (b * nq + i, 0)),
        out_shape=jax.ShapeDtypeStruct((n_seq * seq_len, GROUP_WIDTH), F32),
        scratch_shapes=[
            pltpu.VMEM((2 * ATTN_HEADS, tq, 128), F32),
            pltpu.VMEM((2 * ATTN_HEADS, tq, 256), F32),
        ],
        compiler_params=_params("parallel", "parallel", "arbitrary"),
        name="prompt_attn",
    )(q, k, v, lam_p, subln)


PAGES_PER_STEP = 8


def _decode_attn_kernel(pt_ref, qrows_ref, *refs, n_new, n_steps):
    P = PAGES_PER_STEP
    k_refs = refs[0:P]
    v_refs = refs[P:2 * P]
    knew_ref, vnew_ref, lam_ref, gain_ref, o_ref, m_ref, l_ref, acc_ref = refs[2 * P:]
    j = pl.program_id(1)
    n_rows = 2 * ATTN_HEADS * n_new

    @pl.when(j == 0)
    def _():
        m_ref[...] = jnp.full(m_ref.shape, NEG_BIG, F32)
        l_ref[...] = jnp.zeros(l_ref.shape, F32)
        acc_ref[...] = jnp.zeros(acc_ref.shape, F32)

    qrows = qrows_ref[...]

    def head_values(v_ref, h):
        return v_ref[pl.ds(h, PAGE_SIZE, stride=ATTN_HEADS), :].astype(BF16)

    def update(s_list, v_ref_list):
        m_prev = m_ref[...]
        m_new = m_prev
        for s in s_list:
            m_new = jnp.maximum(m_new, jnp.max(s, axis=-1, keepdims=True))
        alpha = jnp.exp2(m_prev - m_new)
        l_new = alpha * l_ref[...]
        pv = [jnp.zeros((2 * n_new, 128), F32) for _ in range(ATTN_HEADS)]
        for s, v_ref in zip(s_list, v_ref_list):
            p = jnp.exp2(s - m_new)
            l_new = l_new + jnp.sum(p, axis=-1, keepdims=True)
            p_b = p.astype(BF16)
            for h in range(ATTN_HEADS):
                rows = slice(h * 2 * n_new, (h + 1) * 2 * n_new)
                pv[h] = pv[h] + _dot(p_b[rows, :], head_values(v_ref, h))
        for h in range(ATTN_HEADS):
            rows = slice(h * 2 * n_new, (h + 1) * 2 * n_new)
            acc_ref[rows, :] = alpha[rows, :] * acc_ref[rows, :] + pv[h]
        l_ref[...] = l_new
        m_ref[...] = m_new

    @pl.when(j < n_steps)
    def _():
        s_list = [_dot(qrows, k_refs[p][...].astype(BF16)) for p in range(P)]
        update(s_list, list(v_refs))

    @pl.when(j == n_steps)
    def _():
        s = _dot(qrows, knew_ref[...].astype(BF16))
        key = lax.broadcasted_iota(jnp.int32, (n_rows, PAGE_SIZE), 1)
        tok = lax.broadcasted_iota(jnp.int32, (n_rows, PAGE_SIZE), 0) % n_new
        s = jnp.where(key <= tok, s, NEG_BIG)
        update([s], [vnew_ref])
        lam, lam_init = _lam_value(lam_ref)
        gain = gain_ref[...]
        for h in range(ATTN_HEADS):
            r1 = slice(h * 2 * n_new, h * 2 * n_new + n_new)
            r2 = slice(h * 2 * n_new + n_new, (h + 1) * 2 * n_new)
            o = acc_ref[r1, :] / l_ref[r1, :] - lam * (acc_ref[r2, :] / l_ref[r2, :])
            o_ref[:, h * 128:(h + 1) * 128] = _sub_norm(o, gain, lam_init)


def _decode_attn(layer, page_table_flat, qrows, cache_kt, cache_v, knew_t, vnew, lam_p, subln,
                 n_seq, n_new, n_pages):
    P = PAGES_PER_STEP
    n_steps = n_pages // P
    n_rows = 2 * ATTN_HEADS * n_new

    def page_spec(p):
        def index(b, j, pt):
            step = jnp.minimum(j, n_steps - 1)
            return (layer, pt[b * n_pages + step * P + p], 0, 0)
        return pl.BlockSpec((None, None, GROUP_WIDTH, PAGE_SIZE), index)

    seq3 = lambda b, j, pt: (b, 0, 0)
    kern = functools.partial(_decode_attn_kernel, n_new=n_new, n_steps=n_steps)
    grid_spec = pltpu.PrefetchScalarGridSpec(
        num_scalar_prefetch=1,
        grid=(n_seq, n_steps + 1),
        in_specs=(
            [pl.BlockSpec((None, n_rows, GROUP_WIDTH), seq3)]
            + [page_spec(p) for p in range(P)]
            + [page_spec(p) for p in range(P)]
            + [pl.BlockSpec((None, GROUP_WIDTH, PAGE_SIZE), seq3),
               pl.BlockSpec((None, GROUP_WIDTH, PAGE_SIZE), seq3),
               pl.BlockSpec((None, 8, 128), lambda b, j, pt: (layer, 0, 0)),
               pl.BlockSpec((None, 1, 128), lambda b, j, pt: (layer, 0, 0))]
        ),
        out_specs=pl.BlockSpec((n_new, GROUP_WIDTH), lambda b, j, pt: (b, 0)),
        scratch_shapes=[
            pltpu.VMEM((n_rows, 128), F32),
            pltpu.VMEM((n_rows, 128), F32),
            pltpu.VMEM((n_rows, 128), F32),
        ],
    )
    return pl.pallas_call(
        kern,
        grid_spec=grid_spec,
        out_shape=jax.ShapeDtypeStruct((n_seq * n_new, GROUP_WIDTH), F32),
        compiler_params=_params("parallel", "arbitrary"),
        name="decode_attn",
    )(page_table_flat, qrows, *([cache_kt] * P), *([cache_v] * P), knew_t, vnew, lam_p, subln)


def _out_ffn_kernel(h_ref, ys_ref, yp_ref, ya_ref, wo_ref, g_ref, wg_ref, wu_ref, wd_ref,
                    o_ref, acc_ref, u_ref):
    f = pl.program_id(1)

    @pl.when(f == 0)
    def _():
        h1 = h_ref[...]
        h1 = h1 + _dot(ys_ref[...].astype(BF16), wo_ref[0:GROUP_WIDTH, :])
        h1 = h1 + _dot(yp_ref[...].astype(BF16), wo_ref[GROUP_WIDTH:2 * GROUP_WIDTH, :])
        h1 = h1 + _dot(ya_ref[...].astype(BF16), wo_ref[2 * GROUP_WIDTH:3 * GROUP_WIDTH, :])
        acc_ref[...] = h1
        ms = jnp.mean(h1 * h1, axis=-1, keepdims=True)
        u_ref[...] = (h1 * lax.rsqrt(ms + RMS_EPS) * g_ref[...]).astype(BF16)

    u = u_ref[...]
    act = (_silu(_dot(u, wg_ref[...])) * _dot(u, wu_ref[...])).astype(BF16)
    acc_ref[...] += _dot(act, wd_ref[...])

    @pl.when(f == pl.num_programs(1) - 1)
    def _():
        o_ref[...] = acc_ref[...]


def _out_ffn(layer, h, ys, yp, ya, w_out_b, norm_ffn, w_gu_b, w_down_b, tm, tf):
    n_tok, d = h.shape
    ffn = w_down_b.shape[1]
    nf = ffn // tf
    row = lambda i, f: (i, 0)
    return pl.pallas_call(
        _out_ffn_kernel,
        grid=(n_tok // tm, nf),
        in_specs=[
            pl.BlockSpec((tm, d), row),
            pl.BlockSpec((tm, GROUP_WIDTH), row),
            pl.BlockSpec((tm, GROUP_WIDTH), row),
            pl.BlockSpec((tm, GROUP_WIDTH), row),
            pl.BlockSpec((None, 3 * GROUP_WIDTH, d), lambda i, f: (layer, 0, 0)),
            pl.BlockSpec((None, 1, d), lambda i, f: (layer, 0, 0)),
            pl.BlockSpec((None, d, tf), lambda i, f: (layer, 0, f)),
            pl.BlockSpec((None, d, tf), lambda i, f: (layer, 0, nf + f)),
            pl.BlockSpec((None, tf, d), lambda i, f: (layer, f, 0)),
        ],
        out_specs=pl.BlockSpec((tm, d), row),
        out_shape=jax.ShapeDtypeStruct((n_tok, d), F32),
        scratch_shapes=[pltpu.VMEM((tm, d), F32), pltpu.VMEM((tm, d), BF16)],
        compiler_params=_params("parallel", "arbitrary"),
        name="out_ffn",
    )(h, ys, yp, ya, w_out_b, norm_ffn, w_gu_b, w_gu_b, w_down_b)


def _rope_tables(pos):
    half = HEAD_DIM // 2
    inv_freq = ROPE_THETA ** (-jnp.arange(half, dtype=F32) / half)
    ang = pos.astype(F32)[:, None] * inv_freq[None, :]
    return jnp.tile(jnp.cos(ang), (1, 128 // half)), jnp.tile(jnp.sin(ang), (1, 128 // half))


def _pad_rows_front(x, rows):
    pad = jnp.zeros(x.shape[:-2] + (rows - x.shape[-2], x.shape[-1]), x.dtype)
    return jnp.concatenate([pad, x], axis=-2)


def kernel(x_prompt, x_sample, cache_k, cache_v, state_conv, state_ssm, state_pool, page_table, norm_mix, w_in, conv_w, conv_b, dt_bias, a_log, d_skip, ssd_norm, pool_w, pool_scale, q_norm, k_norm, lam_q1, lam_k1, lam_q2, lam_k2, subln, w_out, norm_ffn, w_gate_up, w_down):
    depth = w_in.shape[0]
    bp, lp, d = x_prompt.shape
    bs, ls, _ = x_sample.shape
    n_pages = page_table.shape[1]
    past_len = n_pages * PAGE_SIZE
    ffn = w_down.shape[1]

    c0 = GROUP_WIDTH
    c1 = c0 + GROUP_WIDTH
    c2 = c1 + 256
    c3 = c2 + SSD_HEADS
    w_in_p = jnp.concatenate(
        [w_in[..., 0:c1], w_in[..., c3:], w_in[..., c1:c2], w_in[..., c2:c3],
         jnp.zeros(w_in.shape[:2] + (128 - SSD_HEADS,), w_in.dtype)], axis=-1).astype(BF16)
    w_out_b = w_out.astype(BF16)
    w_gu_b = w_gate_up.astype(BF16)
    w_down_b = w_down.astype(BF16)
    pool_w_b = pool_w.astype(BF16)
    norm_mix3 = norm_mix[:, None, :]
    norm_ffn3 = norm_ffn[:, None, :]
    q_gain = jnp.tile(q_norm, (1, 4))[:, None, :]
    k_gain = jnp.tile(k_norm, (1, 4))[:, None, :]
    blk = jnp.arange(256) // HEAD_DIM
    seg_mat = ((blk[:, None] == blk[None, :]).astype(F32) / HEAD_DIM).astype(BF16)
    cw = jnp.concatenate([conv_w, jnp.zeros((depth, 8 - CONV_TAPS, conv_w.shape[-1]), F32)], axis=1)
    cwx, cwbc = cw[..., :GROUP_WIDTH], cw[..., GROUP_WIDTH:]
    cbiasx = conv_b[:, None, :GROUP_WIDTH]
    cbiasbc = conv_b[:, None, GROUP_WIDTH:]
    pad_heads = lambda x: jnp.concatenate([x, jnp.zeros((depth, 128 - SSD_HEADS), x.dtype)], axis=-1)[:, None, :]
    dtb = pad_heads(dt_bias)
    alog = pad_heads(a_log)
    dskip = jnp.repeat(d_skip, HEAD_DIM, axis=-1)[:, None, :]
    ssd_gain = ssd_norm[:, None, :]
    pool_scale3 = pool_scale[:, None, :]
    subln3 = subln[:, None, :]
    lam_inits = jnp.asarray([0.8 - 0.6 * math.exp(-0.3 * l) for l in range(depth)], F32)
    lam_rows = jnp.stack([lam_q1, lam_k1, lam_q2, lam_k2], axis=1)
    lam_rows = jnp.concatenate([lam_rows, jnp.zeros((depth, 4, 128 - HEAD_DIM), F32)], axis=-1)
    lam_p = jnp.concatenate(
        [lam_rows, jnp.broadcast_to(lam_inits[:, None, None], (depth, 1, 128)),
         jnp.zeros((depth, 3, 128), F32)], axis=1)

    cos_p, sin_p = _rope_tables(jnp.arange(lp, dtype=jnp.int32))
    pos_s = past_len + jnp.arange(ls, dtype=jnp.int32)
    cos_s, sin_s = _rope_tables(jnp.tile(pos_s, bs))

    zc = jnp.zeros((bp, 8, GROUP_WIDTH + 256), F32)
    cb_s = _pad_rows_front(state_conv, 8)
    h0_p = jnp.zeros((bp, HEAD_DIM, GROUP_WIDTH), F32)
    h0_s = jnp.transpose(state_ssm, (0, 1, 4, 2, 3)).reshape(depth, bs, HEAD_DIM, GROUP_WIDTH)
    pb_p = jnp.zeros((bp, 16, GROUP_WIDTH), F32)
    pb_s = _pad_rows_front(state_pool, 16)
    page_flat = page_table.reshape(-1).astype(jnp.int32)
    n_pool = cache_k.shape[1]
    cache_kt = jnp.transpose(cache_k, (0, 1, 3, 4, 5, 2)).reshape(depth, n_pool, GROUP_WIDTH, PAGE_SIZE)
    cache_v2 = cache_v.reshape(depth, n_pool, PAGE_SIZE * ATTN_HEADS, 2 * HEAD_DIM)
    eye_hc = jnp.eye(2 * ATTN_HEADS, dtype=BF16)

    hp = x_prompt.reshape(bp * lp, d)
    hs = x_sample.reshape(bs * ls, d)
    tm_p = 512 if (bp * lp) % 512 == 0 and lp % 512 == 0 else lp
    tm_s = bs * ls
    tf = ffn // 4 if (ffn // 4) % 128 == 0 else ffn

    outs_p = [[], [], [], [], []]
    outs_s = [[], [], [], [], []]

    def state_outputs(k, v, ncx, ncbc, hout, nbuf, n_seq, seq_len, store):
        store[0].append(k.reshape(n_seq, seq_len, ATTN_HEADS, 2, HEAD_DIM))
        store[1].append(v.reshape(n_seq, seq_len, ATTN_HEADS, 2 * HEAD_DIM))
        store[2].append(jnp.concatenate([ncx, ncbc], axis=-1)[:, 8 - (CONV_TAPS - 1):, :])
        store[3].append(jnp.transpose(hout.reshape(n_seq, HEAD_DIM, SSD_HEADS, HEAD_DIM), (0, 2, 3, 1)))
        store[4].append(nbuf[:, 16 - POOL_BUF:, :])

    for l in range(depth):
        ssd_w = (cwx, cwbc, cbiasx, cbiasbc, dtb, alog, dskip, ssd_gain)
        z, xs, pin, qb, k, v, kb, vb, bc, dt = _in_proj(l, hp, norm_mix3, w_in_p, cos_p, sin_p, q_gain, k_gain,
                                                        seg_mat, tm_p)
        y_ssd, ncx, ncbc, hout = _ssd(l, z, xs, bc, dt, zc[..., :GROUP_WIDTH], zc[..., GROUP_WIDTH:], h0_p,
                                      *ssd_w, n_seq=bp, seq_len=lp)
        y_pool, nbuf = _pool(l, pin, pb_p, pool_w_b, pool_scale3, bp, lp, 0)
        y_attn = _prompt_attn(l, qb, kb, vb, lam_p, subln3, bp, lp)
        hp = _out_ffn(l, hp, y_ssd, y_pool, y_attn, w_out_b, norm_ffn3, w_gu_b, w_down_b, tm_p, tf)
        state_outputs(k, v, ncx, ncbc, hout, nbuf, bp, lp, outs_p)

        z, xs, pin, qb, k, v, kb, vb, bc, dt = _in_proj(l, hs, norm_mix3, w_in_p, cos_s, sin_s, q_gain, k_gain,
                                                        seg_mat, tm_s)
        y_ssd, ncx, ncbc, hout = _ssd(l, z, xs, bc, dt, cb_s[l, ..., :GROUP_WIDTH], cb_s[l, ..., GROUP_WIDTH:],
                                      h0_s[l], *ssd_w, n_seq=bs, seq_len=ls)
        y_pool, nbuf = _pool(l, pin, pb_s[l], pool_w_b, pool_scale3, bs, ls, past_len)
        q4 = qb.reshape(bs, ls, 2 * ATTN_HEADS, HEAD_DIM)
        qrows = jnp.transpose(q4[:, :, :, None, :] * eye_hc[None, None, :, :, None], (0, 2, 1, 3, 4))
        qrows = qrows.reshape(bs, 2 * ATTN_HEADS * ls, GROUP_WIDTH)
        knew_t = jnp.transpose(k.reshape(bs, ls, GROUP_WIDTH), (0, 2, 1))
        knew_t = jnp.concatenate([knew_t, jnp.zeros((bs, GROUP_WIDTH, PAGE_SIZE - ls), F32)], axis=2)
        vnew = v.reshape(bs, ls * ATTN_HEADS, 2 * HEAD_DIM)
        vnew = jnp.concatenate([vnew, jnp.zeros((bs, (PAGE_SIZE - ls) * ATTN_HEADS, 2 * HEAD_DIM), F32)], axis=1)
        y_attn = _decode_attn(l, page_flat, qrows, cache_kt, cache_v2, knew_t, vnew, lam_p, subln3,
                              bs, ls, n_pages)
        hs = _out_ffn(l, hs, y_ssd, y_pool, y_attn, w_out_b, norm_ffn3, w_gu_b, w_down_b, tm_s, tf)
        state_outputs(k, v, ncx, ncbc, hout, nbuf, bs, ls, outs_s)

    stacked_p = [jnp.stack(s, axis=0) for s in outs_p]
    stacked_s = [jnp.stack(s, axis=0) for s in outs_s]
    return (hp.reshape(bp, lp, d), hs.reshape(bs, ls, d), *stacked_p, *stacked_s)
```

```python
import functools
import math

import jax
import jax.numpy as jnp
from jax import lax
from jax.experimental import pallas as pl
from jax.experimental.pallas import tpu as pltpu

F32 = jnp.float32
BF16 = jnp.bfloat16

RMS_EPS = 1e-6
ROPE_THETA = 10000.0
HEAD_DIM = 64
GROUP_WIDTH = 512
CONV_TAPS = 4
POOL_WINDOWS = (2, 4, 8, 16)
POOL_BUF = 15
SSD_CHUNK = 128
SSD_HEADS = 8
SSD_GROUPS = 2
ATTN_HEADS = 4
PAGE_SIZE = 128
NEG_BIG = -1e30
QK_SCALE_LOG2 = (HEAD_DIM ** -0.5) * math.log2(math.e)
SAFE_SCORE_BOUND = 64.0

VMEM_LIMIT_BYTES = 56 * 1024 * 1024

OFF_Z, OFF_XS, OFF_POOL, OFF_Q, OFF_K, OFF_V, OFF_BC, OFF_DT = 0, 512, 1024, 1536, 2048, 2560, 3072, 3328
IN_PROJ_PADDED = 3456


def _params(*semantics):
    return pltpu.CompilerParams(dimension_semantics=semantics, vmem_limit_bytes=VMEM_LIMIT_BYTES)


def _silu(x):
    return x / (1.0 + jnp.exp(-x))


def _dot(a, b):
    return jnp.dot(a, b, preferred_element_type=F32)


def _dot_nt(a, b):
    return lax.dot_general(a, b, (((1,), (1,)), ((), ())), preferred_element_type=F32)


def _dot_tn(a, b):
    return lax.dot_general(a, b, (((0,), (0,)), ((), ())), preferred_element_type=F32)


def _split3(x):
    hi = x.astype(BF16)
    r1 = x - hi.astype(F32)
    mid = r1.astype(BF16)
    lo = (r1 - mid.astype(F32)).astype(BF16)
    return hi, mid, lo


def _in_proj_kernel(x_ref, g_ref, w_ref, cos_ref, sin_ref, qg_ref, kg_ref, seg_ref,
                    z_ref, xs_ref, pool_ref, qb_ref, k_ref, v_ref, kb_ref, vb_ref, bc_ref, dt_ref):
    x = x_ref[...]
    ms = jnp.mean(x * x, axis=-1, keepdims=True)
    u = (x * lax.rsqrt(ms + RMS_EPS) * g_ref[...]).astype(BF16)

    def proj(off, width):
        return _dot(u, w_ref[:, off:off + width])

    z_ref[...] = proj(OFF_Z, GROUP_WIDTH)
    xs_ref[...] = proj(OFF_XS, GROUP_WIDTH)
    pool_ref[...] = proj(OFF_POOL, GROUP_WIDTH)
    v = proj(OFF_V, GROUP_WIDTH)
    v_ref[...] = v
    vb_ref[...] = v.astype(BF16)
    bc_ref[...] = proj(OFF_BC, 256)
    dt_ref[...] = proj(OFF_DT, 128)

    cos = cos_ref[...]
    sin = sin_ref[...]
    seg = seg_ref[...]
    lane = lax.broadcasted_iota(jnp.int32, (1, 128), 1)
    first_half = (lane % HEAD_DIM) < (HEAD_DIM // 2)

    def norm_rope(off, gain_ref, out_ref, out_b_ref, scale):
        gain = gain_ref[...]
        for s in range(GROUP_WIDTH // 256):
            y = proj(off + s * 256, 256)
            seg_ms = _dot((y * y).astype(BF16), seg)
            yn_wide = y * lax.rsqrt(seg_ms + RMS_EPS) * gain
            for t in range(2):
                lo = s * 256 + t * 128
                yn = yn_wide[:, t * 128:(t + 1) * 128]
                rot = jnp.where(first_half, -pltpu.roll(yn, 128 - HEAD_DIM // 2, 1),
                                pltpu.roll(yn, HEAD_DIM // 2, 1))
                roped = yn * cos + rot * sin
                if out_ref is not None:
                    out_ref[:, lo:lo + 128] = roped
                out_b_ref[:, lo:lo + 128] = (roped * scale).astype(BF16)

    norm_rope(OFF_Q, qg_ref, None, qb_ref, QK_SCALE_LOG2)
    norm_rope(OFF_K, kg_ref, k_ref, kb_ref, 1.0)


def _in_proj(layer, x, norm_mix, w_in_p, cos_t, sin_t, q_gain, k_gain, seg_mat, tm):
    n_tok = x.shape[0]
    n_pos_blocks = cos_t.shape[0] // tm
    row = lambda i: (i, 0)
    widths = (GROUP_WIDTH,) * 8 + (256, 128)
    dtypes = (F32, F32, F32, BF16, F32, F32, BF16, BF16, F32, F32)
    return pl.pallas_call(
        _in_proj_kernel,
        grid=(n_tok // tm,),
        in_specs=[
            pl.BlockSpec((tm, x.shape[1]), row),
            pl.BlockSpec((None, 1, x.shape[1]), lambda i: (layer, 0, 0)),
            pl.BlockSpec((None, x.shape[1], IN_PROJ_PADDED), lambda i: (layer, 0, 0)),
            pl.BlockSpec((tm, 128), lambda i: (i % n_pos_blocks, 0)),
            pl.BlockSpec((tm, 128), lambda i: (i % n_pos_blocks, 0)),
            pl.BlockSpec((None, 1, 256), lambda i: (layer, 0, 0)),
            pl.BlockSpec((None, 1, 256), lambda i: (layer, 0, 0)),
            pl.BlockSpec((256, 256), lambda i: (0, 0)),
        ],
        out_specs=[pl.BlockSpec((tm, w), row) for w in widths],
        out_shape=[jax.ShapeDtypeStruct((n_tok, w), dt) for w, dt in zip(widths, dtypes)],
        compiler_params=_params("parallel"),
        name="in_proj",
    )(x, norm_mix, w_in_p, cos_t, sin_t, q_gain, k_gain, seg_mat)


def _ssd_kernel(z_ref, xs_ref, bc_ref, dt_ref, cbx_ref, cbbc_ref, h0_ref,
                cwx_ref, cwbc_ref, cbiasx_ref, cbiasbc_ref, dtb_ref, alog_ref, dskip_ref, gain_ref,
                y_ref, ncx_ref, ncbc_ref, hout_ref,
                extx_ref, extbc_ref, h_ref, *, valid):
    Q = SSD_CHUNK
    c = pl.program_id(1)

    @pl.when(c == 0)
    def _():
        extx_ref[0:8, :] = cbx_ref[...]
        extbc_ref[0:8, :] = cbbc_ref[...]
        h_ref[...] = h0_ref[...]
        if valid < Q:
            extx_ref[8 + valid:8 + Q, :] = jnp.zeros((Q - valid, GROUP_WIDTH), F32)
            extbc_ref[8 + valid:8 + Q, :] = jnp.zeros((Q - valid, 256), F32)

    extx_ref[8:8 + valid, :] = xs_ref[...]
    extbc_ref[8:8 + valid, :] = bc_ref[...]

    def conv(ext_ref, w_ref, b_ref):
        acc = b_ref[...] + ext_ref[8:8 + Q, :] * w_ref[CONV_TAPS - 1:CONV_TAPS, :]
        for i in range(CONV_TAPS - 1):
            lo = 8 - (CONV_TAPS - 1) + i
            acc = acc + ext_ref[lo:lo + Q, :] * w_ref[i:i + 1, :]
        return _silu(acc)

    xs = conv(extx_ref, cwx_ref, cbiasx_ref)
    bc = conv(extbc_ref, cwbc_ref, cbiasbc_ref)
    tailx = extx_ref[valid:valid + 8, :]
    tailbc = extbc_ref[valid:valid + 8, :]
    extx_ref[0:8, :] = tailx
    extbc_ref[0:8, :] = tailbc
    ncx_ref[...] = tailx
    ncbc_ref[...] = tailbc

    dtr = dt_ref[...] + dtb_ref[...]
    dt = jnp.maximum(dtr, 0.0) + jnp.log1p(jnp.exp(-jnp.abs(dtr)))
    if valid < Q:
        dt = jnp.concatenate([dt, jnp.zeros((Q - valid, 128), F32)], axis=0)
    a = -jnp.exp(alog_ref[...])
    d_a = dt * a
    ri = lax.broadcasted_iota(jnp.int32, (Q, Q), 0)
    ci = lax.broadcasted_iota(jnp.int32, (Q, Q), 1)
    causal = ri >= ci
    tri = causal.astype(BF16)
    hi, mid, lo = _split3(d_a)
    cum = _dot(tri, hi) + _dot(tri, mid) + _dot(tri, lo)
    cum_t = cum.T
    dt_t = dt.T
    cum_last = cum[Q - 1:Q, :]
    e_cum = jnp.exp(cum)
    to_end = jnp.exp(cum_last - cum) * dt
    e_last = jnp.exp(cum_last)

    xs_b = xs.astype(BF16)
    h_prev = h_ref[...]
    h_prev_b = h_prev.astype(BF16)
    heads_per_group = SSD_HEADS // SSD_GROUPS
    gw = heads_per_group * HEAD_DIM
    y_parts = []
    xw_parts = []
    dec_parts = []
    for g in range(SSD_GROUPS):
        b_g = bc[:, g * HEAD_DIM:(g + 1) * HEAD_DIM].astype(BF16)
        c_g = bc[:, 128 + g * HEAD_DIM:128 + (g + 1) * HEAD_DIM].astype(BF16)
        cb = _dot_nt(c_g, b_g)
        y_off_g = _dot(c_g, h_prev_b[:, g * gw:(g + 1) * gw])
        for hh in range(heads_per_group):
            h = g * heads_per_group + hh
            seg = cum[:, h:h + 1] - cum_t[h:h + 1, :]
            decay = jnp.exp(jnp.where(causal, seg, NEG_BIG))
            scores = (cb * decay * dt_t[h:h + 1, :]).astype(BF16)
            x_h = xs_b[:, h * HEAD_DIM:(h + 1) * HEAD_DIM]
            y_h = _dot(scores, x_h)
            y_h = y_h + y_off_g[:, hh * HEAD_DIM:(hh + 1) * HEAD_DIM] * e_cum[:, h:h + 1]
            y_h = y_h + dskip_ref[:, h * HEAD_DIM:(h + 1) * HEAD_DIM] * xs[:, h * HEAD_DIM:(h + 1) * HEAD_DIM]
            y_parts.append(y_h)
            xw_parts.append(xs[:, h * HEAD_DIM:(h + 1) * HEAD_DIM] * to_end[:, h:h + 1])
            dec_parts.append(jnp.broadcast_to(e_last[:, h:h + 1], (1, HEAD_DIM)))
        xw_g = jnp.concatenate(xw_parts[-heads_per_group:], axis=1).astype(BF16)
        st_g = _dot_tn(b_g, xw_g)
        dec_g = jnp.concatenate(dec_parts[-heads_per_group:], axis=1)
        h_ref[:, g * gw:(g + 1) * gw] = dec_g * h_prev[:, g * gw:(g + 1) * gw] + st_g
    hout_ref[...] = h_ref[...]

    y = jnp.concatenate(y_parts, axis=1)[0:valid, :]
    y = y * _silu(z_ref[...])
    ms = jnp.mean(y * y, axis=-1, keepdims=True)
    y_ref[...] = y * lax.rsqrt(ms + RMS_EPS) * gain_ref[...]


def _ssd(layer, z, xs, bc, dt, cbx, cbbc, h0t, cwx, cwbc, cbiasx, cbiasbc, dtb, alog, dskip, gain,
         n_seq, seq_len):
    valid = min(seq_len, SSD_CHUNK)
    n_chunks = seq_len // valid
    tok = lambda b, c: (b * n_chunks + c, 0)
    seq3 = lambda b, c: (b, 0, 0)
    lay3 = lambda b, c: (layer, 0, 0)
    n_tok = n_seq * seq_len
    kern = functools.partial(_ssd_kernel, valid=valid)
    return pl.pallas_call(
        kern,
        grid=(n_seq, n_chunks),
        in_specs=[
            pl.BlockSpec((valid, GROUP_WIDTH), tok),
            pl.BlockSpec((valid, GROUP_WIDTH), tok),
            pl.BlockSpec((valid, 256), tok),
            pl.BlockSpec((valid, 128), tok),
            pl.BlockSpec((None, 8, GROUP_WIDTH), seq3),
            pl.BlockSpec((None, 8, 256), seq3),
            pl.BlockSpec((None, HEAD_DIM, GROUP_WIDTH), seq3),
            pl.BlockSpec((None, 8, GROUP_WIDTH), lay3),
            pl.BlockSpec((None, 8, 256), lay3),
            pl.BlockSpec((None, 1, GROUP_WIDTH), lay3),
            pl.BlockSpec((None, 1, 256), lay3),
            pl.BlockSpec((None, 1, 128), lay3),
            pl.BlockSpec((None, 1, 128), lay3),
            pl.BlockSpec((None, 1, GROUP_WIDTH), lay3),
            pl.BlockSpec((None, 1, GROUP_WIDTH), lay3),
        ],
        out_specs=[
            pl.BlockSpec((valid, GROUP_WIDTH), tok),
            pl.BlockSpec((None, 8, GROUP_WIDTH), seq3),
            pl.BlockSpec((None, 8, 256), seq3),
            pl.BlockSpec((None, HEAD_DIM, GROUP_WIDTH), seq3),
        ],
        out_shape=[
            jax.ShapeDtypeStruct((n_tok, GROUP_WIDTH), F32),
            jax.ShapeDtypeStruct((n_seq, 8, GROUP_WIDTH), F32),
            jax.ShapeDtypeStruct((n_seq, 8, 256), F32),
            jax.ShapeDtypeStruct((n_seq, HEAD_DIM, GROUP_WIDTH), F32),
        ],
        scratch_shapes=[
            pltpu.VMEM((8 + SSD_CHUNK, GROUP_WIDTH), F32),
            pltpu.VMEM((8 + SSD_CHUNK, 256), F32),
            pltpu.VMEM((HEAD_DIM, GROUP_WIDTH), F32),
        ],
        compiler_params=_params("parallel", "arbitrary"),
        name="ssd",
    )(z, xs, bc, dt, cbx, cbbc, h0t, cwx, cwbc, cbiasx, cbiasbc, dtb, alog, dskip, gain)


def _pool_kernel(x_ref, buf_ref, w_ref, scale_ref, y_ref, nbuf_ref, ext_ref, *, tl, pos0):
    c = pl.program_id(1)

    @pl.when(c == 0)
    def _():
        ext_ref[0:16, :] = buf_ref[...]

    ext_ref[16:16 + tl, :] = x_ref[...]
    row = lax.broadcasted_iota(jnp.int32, (tl, 1), 0)
    pos = pos0 + c * tl + row
    for g, w in enumerate(POOL_WINDOWS):
        sl = slice(g * 128, (g + 1) * 128)
        cur = ext_ref[16:16 + tl, sl]
        win = cur
        for k in range(1, w):
            win = win + ext_ref[16 - k:16 - k + tl, sl]
        cnt = jnp.minimum(w, pos + 1).astype(F32)
        pooled = (win / cnt - cur).astype(BF16)
        y_ref[:, sl] = _dot(pooled, w_ref[g]) * scale_ref[:, sl]
    tail = ext_ref[tl:tl + 16, :]
    ext_ref[0:16, :] = tail
    nbuf_ref[...] = tail


def _pool(layer, x, buf, pool_w, pool_scale, n_seq, seq_len, pos0):
    tl = min(seq_len, 256)
    n_chunks = seq_len // tl
    tok = lambda b, c: (b * n_chunks + c, 0)
    kern = functools.partial(_pool_kernel, tl=tl, pos0=pos0)
    return pl.pallas_call(
        kern,
        grid=(n_seq, n_chunks),
        in_specs=[
            pl.BlockSpec((tl, GROUP_WIDTH), tok),
            pl.BlockSpec((None, 16, GROUP_WIDTH), lambda b, c: (b, 0, 0)),
            pl.BlockSpec((None, 4, 128, 128), lambda b, c: (layer, 0, 0, 0)),
            pl.BlockSpec((None, 1, GROUP_WIDTH), lambda b, c: (layer, 0, 0)),
        ],
        out_specs=[
            pl.BlockSpec((tl, GROUP_WIDTH), tok),
            pl.BlockSpec((None, 16, GROUP_WIDTH), lambda b, c: (b, 0, 0)),
        ],
        out_shape=[
            jax.ShapeDtypeStruct((n_seq * seq_len, GROUP_WIDTH), F32),
            jax.ShapeDtypeStruct((n_seq, 16, GROUP_WIDTH), F32),
        ],
        scratch_shapes=[pltpu.VMEM((16 + tl, GROUP_WIDTH), F32)],
        compiler_params=_params("parallel", "arbitrary"),
        name="pool",
    )(x, buf, pool_w, pool_scale)


def _lam_value(lam_ref):
    lp = lam_ref[...]
    s1 = jnp.sum(lp[0:1, :] * lp[1:2, :], axis=-1, keepdims=True)
    s2 = jnp.sum(lp[2:3, :] * lp[3:4, :], axis=-1, keepdims=True)
    lam_init = lp[4:5, 0:1]
    return jnp.exp(s1) - jnp.exp(s2) + lam_init, lam_init


def _sub_norm(o, gain, lam_init):
    ms = jnp.mean(o * o, axis=-1, keepdims=True)
    return o * lax.rsqrt(ms + RMS_EPS) * gain * (1.0 - lam_init)


ATTN_BLOCK = 256


def _prompt_attn_kernel(flag_ref, q_ref, k_ref, v_ref, lam_ref, gain_ref, o_ref, m_ref, acc_ref, *, layer):
    T = ATTN_BLOCK
    i = pl.program_id(1)
    ones = jnp.ones((T, 128), BF16)

    def scores(j, hc, masked):
        start = pl.multiple_of(j * T, T)
        sl = slice(hc * HEAD_DIM, (hc + 1) * HEAD_DIM)
        s = _dot_nt(q_ref[:, sl], k_ref[pl.ds(start, T), sl])
        if masked:
            ri = lax.broadcasted_iota(jnp.int32, (T, T), 0)
            ci = lax.broadcasted_iota(jnp.int32, (T, T), 1)
            s = jnp.where(ci <= ri, s, NEG_BIG)
        return s

    m_ref[...] = jnp.zeros(m_ref.shape, F32)

    @pl.when(flag_ref[layer] == 0)
    def _():
        def fold_max(j, masked):
            for hc in range(2 * ATTN_HEADS):
                s = scores(j, hc, masked)
                m_ref[hc] = jnp.maximum(m_ref[hc], jnp.maximum(s[:, :128], s[:, 128:]))

        m_ref[...] = jnp.full(m_ref.shape, NEG_BIG, F32)
        fold_max(i, True)

        def body(j, carry):
            fold_max(j, False)
            return carry
        lax.fori_loop(0, i, body, 0)
        for hc in range(2 * ATTN_HEADS):
            m_ref[hc] = jnp.broadcast_to(jnp.max(m_ref[hc], axis=-1, keepdims=True), (T, 128))

    def accumulate(j, masked, first):
        start = pl.multiple_of(j * T, T)
        for h in range(ATTN_HEADS):
            v_aug = jnp.concatenate([v_ref[pl.ds(start, T), h * 128:(h + 1) * 128], ones], axis=1)
            for c in range(2):
                hc = 2 * h + c
                shift = jnp.tile(m_ref[hc], (1, T // 128))
                p = jnp.exp2(scores(j, hc, masked) - shift).astype(BF16)
                pv = _dot(p, v_aug)
                if first:
                    acc_ref[hc] = pv
                else:
                    acc_ref[hc] += pv

    accumulate(i, True, True)

    def body(j, carry):
        accumulate(j, False, False)
        return carry
    lax.fori_loop(0, i, body, 0)

    lam, lam_init = _lam_value(lam_ref)
    gain = gain_ref[...]
    for h in range(ATTN_HEADS):
        a1 = acc_ref[2 * h]
        a2 = acc_ref[2 * h + 1]
        o = a1[:, :128] / a1[:, 128:] - lam * (a2[:, :128] / a2[:, 128:])
        o_ref[:, h * 128:(h + 1) * 128] = _sub_norm(o, gain, lam_init)


def _prompt_attn(layer, safe_flags, q, k, v, lam_p, subln, n_seq, seq_len):
    T = ATTN_BLOCK
    nq = seq_len // T
    kern = functools.partial(_prompt_attn_kernel, layer=layer)
    grid_spec = pltpu.PrefetchScalarGridSpec(
        num_scalar_prefetch=1,
        grid=(n_seq, nq),
        in_specs=[
            pl.BlockSpec((T, GROUP_WIDTH), lambda b, i, f: (b * nq + i, 0)),
            pl.BlockSpec((seq_len, GROUP_WIDTH), lambda b, i, f: (b, 0)),
            pl.BlockSpec((seq_len, GROUP_WIDTH), lambda b, i, f: (b, 0)),
            pl.BlockSpec((None, 8, 128), lambda b, i, f: (layer, 0, 0)),
            pl.BlockSpec((None, 1, 128), lambda b, i, f: (layer, 0, 0)),
        ],
        out_specs=pl.BlockSpec((T, GROUP_WIDTH), lambda b, i, f: (b * nq + i, 0)),
        scratch_shapes=[
            pltpu.VMEM((2 * ATTN_HEADS, T, 128), F32),
            pltpu.VMEM((2 * ATTN_HEADS, T, 256), F32),
        ],
    )
    return pl.pallas_call(
        kern,
        grid_spec=grid_spec,
        out_shape=jax.ShapeDtypeStruct((n_seq * seq_len, GROUP_WIDTH), F32),
        compiler_params=_params("parallel", "arbitrary"),
        name="prompt_attn",
    )(safe_flags, q, k, v, lam_p, subln)


PAGES_PER_STEP = 16


def _decode_attn_kernel(pt_ref, qrows_ref, *refs, n_new, n_steps):
    P = PAGES_PER_STEP
    k_refs = refs[0:P]
    v_refs = refs[P:2 * P]
    knew_ref, vnew_ref, lam_ref, gain_ref, o_ref, m_ref, l_ref, acc_ref = refs[2 * P:]
    j = pl.program_id(1)
    n_rows = 2 * ATTN_HEADS * n_new

    @pl.when(j == 0)
    def _():
        m_ref[...] = jnp.full(m_ref.shape, NEG_BIG, F32)
        l_ref[...] = jnp.zeros(l_ref.shape, F32)
        acc_ref[...] = jnp.zeros(acc_ref.shape, F32)

    qrows = qrows_ref[...]

    def head_values(v_ref, h):
        return v_ref[pl.ds(h, PAGE_SIZE, stride=ATTN_HEADS), :].astype(BF16)

    def update(s_list, v_ref_list):
        m_prev = m_ref[...]
        m_new = m_prev
        for s in s_list:
            m_new = jnp.maximum(m_new, jnp.max(s, axis=-1, keepdims=True))
        alpha = jnp.exp2(m_prev - m_new)
        l_new = alpha * l_ref[...]
        pv = [jnp.zeros((2 * n_new, 128), F32) for _ in range(ATTN_HEADS)]
        for s, v_ref in zip(s_list, v_ref_list):
            p = jnp.exp2(s - m_new)
            l_new = l_new + jnp.sum(p, axis=-1, keepdims=True)
            p_b = p.astype(BF16)
            for h in range(ATTN_HEADS):
                rows = slice(h * 2 * n_new, (h + 1) * 2 * n_new)
                pv[h] = pv[h] + _dot(p_b[rows, :], head_values(v_ref, h))
        for h in range(ATTN_HEADS):
            rows = slice(h * 2 * n_new, (h + 1) * 2 * n_new)
            acc_ref[rows, :] = alpha[rows, :] * acc_ref[rows, :] + pv[h]
        l_ref[...] = l_new
        m_ref[...] = m_new

    @pl.when(j < n_steps)
    def _():
        s_list = [_dot(qrows, k_refs[p][...].astype(BF16)) for p in range(P)]
        update(s_list, list(v_refs))

    @pl.when(j == n_steps)
    def _():
        s = _dot(qrows, knew_ref[...].astype(BF16))
        key = lax.broadcasted_iota(jnp.int32, (n_rows, PAGE_SIZE), 1)
        tok = lax.broadcasted_iota(jnp.int32, (n_rows, PAGE_SIZE), 0) % n_new
        s = jnp.where(key <= tok, s, NEG_BIG)
        update([s], [vnew_ref])
        lam, lam_init = _lam_value(lam_ref)
        gain = gain_ref[...]
        for h in range(ATTN_HEADS):
            r1 = slice(h * 2 * n_new, h * 2 * n_new + n_new)
            r2 = slice(h * 2 * n_new + n_new, (h + 1) * 2 * n_new)
            o = acc_ref[r1, :] / l_ref[r1, :] - lam * (acc_ref[r2, :] / l_ref[r2, :])
            o_ref[:, h * 128:(h + 1) * 128] = _sub_norm(o, gain, lam_init)


def _decode_attn(layer, page_table_flat, qrows, cache_kt, cache_v, knew_t, vnew, lam_p, subln,
                 n_seq, n_new, n_pages):
    P = PAGES_PER_STEP
    n_steps = n_pages // P
    n_rows = 2 * ATTN_HEADS * n_new

    def page_spec(p):
        def index(b, j, pt):
            step = jnp.minimum(j, n_steps - 1)
            return (layer, pt[b * n_pages + step * P + p], 0, 0)
        return pl.BlockSpec((None, None, GROUP_WIDTH, PAGE_SIZE), index)

    seq3 = lambda b, j, pt: (b, 0, 0)
    kern = functools.partial(_decode_attn_kernel, n_new=n_new, n_steps=n_steps)
    grid_spec = pltpu.PrefetchScalarGridSpec(
        num_scalar_prefetch=1,
        grid=(n_seq, n_steps + 1),
        in_specs=(
            [pl.BlockSpec((None, n_rows, GROUP_WIDTH), seq3)]
            + [page_spec(p) for p in range(P)]
            + [page_spec(p) for p in range(P)]
            + [pl.BlockSpec((None, GROUP_WIDTH, PAGE_SIZE), seq3),
               pl.BlockSpec((None, GROUP_WIDTH, PAGE_SIZE), seq3),
               pl.BlockSpec((None, 8, 128), lambda b, j, pt: (layer, 0, 0)),
               pl.BlockSpec((None, 1, 128), lambda b, j, pt: (layer, 0, 0))]
        ),
        out_specs=pl.BlockSpec((n_new, GROUP_WIDTH), lambda b, j, pt: (b, 0)),
        scratch_shapes=[
            pltpu.VMEM((n_rows, 128), F32),
            pltpu.VMEM((n_rows, 128), F32),
            pltpu.VMEM((n_rows, 128), F32),
        ],
    )
    return pl.pallas_call(
        kern,
        grid_spec=grid_spec,
        out_shape=jax.ShapeDtypeStruct((n_seq * n_new, GROUP_WIDTH), F32),
        compiler_params=_params("parallel", "arbitrary"),
        name="decode_attn",
    )(page_table_flat, qrows, *([cache_kt] * P), *([cache_v] * P), knew_t, vnew, lam_p, subln)


def _out_ffn_kernel(h_ref, ys_ref, yp_ref, ya_ref, wo_ref, g_ref, wg_ref, wu_ref, wd_ref,
                    o_ref, acc_ref, u_ref):
    f = pl.program_id(1)

    @pl.when(f == 0)
    def _():
        h1 = h_ref[...]
        h1 = h1 + _dot(ys_ref[...].astype(BF16), wo_ref[0:GROUP_WIDTH, :])
        h1 = h1 + _dot(yp_ref[...].astype(BF16), wo_ref[GROUP_WIDTH:2 * GROUP_WIDTH, :])
        h1 = h1 + _dot(ya_ref[...].astype(BF16), wo_ref[2 * GROUP_WIDTH:3 * GROUP_WIDTH, :])
        acc_ref[...] = h1
        ms = jnp.mean(h1 * h1, axis=-1, keepdims=True)
        u_ref[...] = (h1 * lax.rsqrt(ms + RMS_EPS) * g_ref[...]).astype(BF16)

    u = u_ref[...]
    act = (_silu(_dot(u, wg_ref[...])) * _dot(u, wu_ref[...])).astype(BF16)
    acc_ref[...] += _dot(act, wd_ref[...])

    @pl.when(f == pl.num_programs(1) - 1)
    def _():
        o_ref[...] = acc_ref[...]


def _out_ffn(layer, h, ys, yp, ya, w_out_b, norm_ffn, w_gu_b, w_down_b, tm, tf):
    n_tok, d = h.shape
    ffn = w_down_b.shape[1]
    nf = ffn // tf
    row = lambda i, f: (i, 0)
    return pl.pallas_call(
        _out_ffn_kernel,
        grid=(n_tok // tm, nf),
        in_specs=[
            pl.BlockSpec((tm, d), row),
            pl.BlockSpec((tm, GROUP_WIDTH), row),
            pl.BlockSpec((tm, GROUP_WIDTH), row),
            pl.BlockSpec((tm, GROUP_WIDTH), row),
            pl.BlockSpec((None, 3 * GROUP_WIDTH, d), lambda i, f: (layer, 0, 0)),
            pl.BlockSpec((None, 1, d), lambda i, f: (layer, 0, 0)),
            pl.BlockSpec((None, d, tf), lambda i, f: (layer, 0, f)),
            pl.BlockSpec((None, d, tf), lambda i, f: (layer, 0, nf + f)),
            pl.BlockSpec((None, tf, d), lambda i, f: (layer, f, 0)),
        ],
        out_specs=pl.BlockSpec((tm, d), row),
        out_shape=jax.ShapeDtypeStruct((n_tok, d), F32),
        scratch_shapes=[pltpu.VMEM((tm, d), F32), pltpu.VMEM((tm, d), BF16)],
        compiler_params=_params("parallel", "arbitrary"),
        name="out_ffn",
    )(h, ys, yp, ya, w_out_b, norm_ffn, w_gu_b, w_gu_b, w_down_b)


def _rope_tables(pos):
    half = HEAD_DIM // 2
    inv_freq = ROPE_THETA ** (-jnp.arange(half, dtype=F32) / half)
    ang = pos.astype(F32)[:, None] * inv_freq[None, :]
    return jnp.tile(jnp.cos(ang), (1, 128 // half)), jnp.tile(jnp.sin(ang), (1, 128 // half))


def _pad_rows_front(x, rows):
    pad = jnp.zeros(x.shape[:-2] + (rows - x.shape[-2], x.shape[-1]), x.dtype)
    return jnp.concatenate([pad, x], axis=-2)


def kernel(x_prompt, x_sample, cache_k, cache_v, state_conv, state_ssm, state_pool, page_table, norm_mix, w_in, conv_w, conv_b, dt_bias, a_log, d_skip, ssd_norm, pool_w, pool_scale, q_norm, k_norm, lam_q1, lam_k1, lam_q2, lam_k2, subln, w_out, norm_ffn, w_gate_up, w_down):
    depth = w_in.shape[0]
    bp, lp, d = x_prompt.shape
    bs, ls, _ = x_sample.shape
    n_pages = page_table.shape[1]
    past_len = n_pages * PAGE_SIZE
    ffn = w_down.shape[1]

    c0 = GROUP_WIDTH
    c1 = c0 + GROUP_WIDTH
    c2 = c1 + 256
    c3 = c2 + SSD_HEADS
    w_in_p = jnp.concatenate(
        [w_in[..., 0:c1], w_in[..., c3:], w_in[..., c1:c2], w_in[..., c2:c3],
         jnp.zeros(w_in.shape[:2] + (128 - SSD_HEADS,), w_in.dtype)], axis=-1).astype(BF16)
    w_out_b = w_out.astype(BF16)
    w_gu_b = w_gate_up.astype(BF16)
    w_down_b = w_down.astype(BF16)
    pool_w_b = pool_w.astype(BF16)
    norm_mix3 = norm_mix[:, None, :]
    norm_ffn3 = norm_ffn[:, None, :]
    q_gain = jnp.tile(q_norm, (1, 4))[:, None, :]
    k_gain = jnp.tile(k_norm, (1, 4))[:, None, :]
    blk = jnp.arange(256) // HEAD_DIM
    seg_mat = ((blk[:, None] == blk[None, :]).astype(F32) / HEAD_DIM).astype(BF16)
    cw = jnp.concatenate([conv_w, jnp.zeros((depth, 8 - CONV_TAPS, conv_w.shape[-1]), F32)], axis=1)
    cwx, cwbc = cw[..., :GROUP_WIDTH], cw[..., GROUP_WIDTH:]
    cbiasx = conv_b[:, None, :GROUP_WIDTH]
    cbiasbc = conv_b[:, None, GROUP_WIDTH:]
    pad_heads = lambda x: jnp.concatenate([x, jnp.zeros((depth, 128 - SSD_HEADS), x.dtype)], axis=-1)[:, None, :]
    dtb = pad_heads(dt_bias)
    alog = pad_heads(a_log)
    dskip = jnp.repeat(d_skip, HEAD_DIM, axis=-1)[:, None, :]
    ssd_gain = ssd_norm[:, None, :]
    pool_scale3 = pool_scale[:, None, :]
    subln3 = subln[:, None, :]
    lam_inits = jnp.asarray([0.8 - 0.6 * math.exp(-0.3 * l) for l in range(depth)], F32)
    lam_rows = jnp.stack([lam_q1, lam_k1, lam_q2, lam_k2], axis=1)
    lam_rows = jnp.concatenate([lam_rows, jnp.zeros((depth, 4, 128 - HEAD_DIM), F32)], axis=-1)
    lam_p = jnp.concatenate(
        [lam_rows, jnp.broadcast_to(lam_inits[:, None, None], (depth, 1, 128)),
         jnp.zeros((depth, 3, 128), F32)], axis=1)

    score_bound = HEAD_DIM * QK_SCALE_LOG2 * jnp.max(jnp.abs(q_norm), axis=-1) * jnp.max(jnp.abs(k_norm), axis=-1)
    safe_flags = (score_bound < SAFE_SCORE_BOUND).astype(jnp.int32)

    cos_p, sin_p = _rope_tables(jnp.arange(lp, dtype=jnp.int32))
    pos_s = past_len + jnp.arange(ls, dtype=jnp.int32)
    cos_s, sin_s = _rope_tables(jnp.tile(pos_s, bs))

    zc = jnp.zeros((bp, 8, GROUP_WIDTH + 256), F32)
    cb_s = _pad_rows_front(state_conv, 8)
    h0_p = jnp.zeros((bp, HEAD_DIM, GROUP_WIDTH), F32)
    h0_s = jnp.transpose(state_ssm, (0, 1, 4, 2, 3)).reshape(depth, bs, HEAD_DIM, GROUP_WIDTH)
    pb_p = jnp.zeros((bp, 16, GROUP_WIDTH), F32)
    pb_s = _pad_rows_front(state_pool, 16)
    page_flat = page_table.reshape(-1).astype(jnp.int32)
    n_pool = cache_k.shape[1]
    cache_kt = jnp.transpose(cache_k, (0, 1, 3, 4, 5, 2)).reshape(depth, n_pool, GROUP_WIDTH, PAGE_SIZE)
    cache_v2 = cache_v.reshape(depth, n_pool, PAGE_SIZE * ATTN_HEADS, 2 * HEAD_DIM)
    eye_hc = jnp.eye(2 * ATTN_HEADS, dtype=BF16)

    hp = x_prompt.reshape(bp * lp, d)
    hs = x_sample.reshape(bs * ls, d)
    tm_p = 512 if (bp * lp) % 512 == 0 and lp % 512 == 0 else lp
    tm_s = bs * ls
    tf = ffn // 4 if (ffn // 4) % 128 == 0 else ffn

    outs_p = [[], [], [], [], []]
    outs_s = [[], [], [], [], []]

    def state_outputs(k, v, ncx, ncbc, hout, nbuf, n_seq, seq_len, store):
        store[0].append(k.reshape(n_seq, seq_len, ATTN_HEADS, 2, HEAD_DIM))
        store[1].append(v.reshape(n_seq, seq_len, ATTN_HEADS, 2 * HEAD_DIM))
        store[2].append(jnp.concatenate([ncx, ncbc], axis=-1)[:, 8 - (CONV_TAPS - 1):, :])
        store[3].append(jnp.transpose(hout.reshape(n_seq, HEAD_DIM, SSD_HEADS, HEAD_DIM), (0, 2, 3, 1)))
        store[4].append(nbuf[:, 16 - POOL_BUF:, :])

    for l in range(depth):
        ssd_w = (cwx, cwbc, cbiasx, cbiasbc, dtb, alog, dskip, ssd_gain)
        z, xs, pin, qb, k, v, kb, vb, bc, dt = _in_proj(l, hp, norm_mix3, w_in_p, cos_p, sin_p, q_gain, k_gain,
                                                        seg_mat, tm_p)
        y_ssd, ncx, ncbc, hout = _ssd(l, z, xs, bc, dt, zc[..., :GROUP_WIDTH], zc[..., GROUP_WIDTH:], h0_p,
                                      *ssd_w, n_seq=bp, seq_len=lp)
        y_pool, nbuf = _pool(l, pin, pb_p, pool_w_b, pool_scale3, bp, lp, 0)
        y_attn = _prompt_attn(l, safe_flags, qb, kb, vb, lam_p, subln3, bp, lp)
        hp = _out_ffn(l, hp, y_ssd, y_pool, y_attn, w_out_b, norm_ffn3, w_gu_b, w_down_b, tm_p, tf)
        state_outputs(k, v, ncx, ncbc, hout, nbuf, bp, lp, outs_p)

        z, xs, pin, qb, k, v, kb, vb, bc, dt = _in_proj(l, hs, norm_mix3, w_in_p, cos_s, sin_s, q_gain, k_gain,
                                                        seg_mat, tm_s)
        y_ssd, ncx, ncbc, hout = _ssd(l, z, xs, bc, dt, cb_s[l, ..., :GROUP_WIDTH], cb_s[l, ..., GROUP_WIDTH:],
                                      h0_s[l], *ssd_w, n_seq=bs, seq_len=ls)
        y_pool, nbuf = _pool(l, pin, pb_s[l], pool_w_b, pool_scale3, bs, ls, past_len)
        q4 = qb.reshape(bs, ls, 2 * ATTN_HEADS, HEAD_DIM)
        qrows = jnp.transpose(q4[:, :, :, None, :] * eye_hc[None, None, :, :, None], (0, 2, 1, 3, 4))
        qrows = qrows.reshape(bs, 2 * ATTN_HEADS * ls, GROUP_WIDTH)
        knew_t = jnp.transpose(k.reshape(bs, ls, GROUP_WIDTH), (0, 2, 1))
        knew_t = jnp.concatenate([knew_t, jnp.zeros((bs, GROUP_WIDTH, PAGE_SIZE - ls), F32)], axis=2)
        vnew = v.reshape(bs, ls * ATTN_HEADS, 2 * HEAD_DIM)
        vnew = jnp.concatenate([vnew, jnp.zeros((bs, (PAGE_SIZE - ls) * ATTN_HEADS, 2 * HEAD_DIM), F32)], axis=1)
        y_attn = _decode_attn(l, page_flat, qrows, cache_kt, cache_v2, knew_t, vnew, lam_p, subln3,
                              bs, ls, n_pages)
        hs = _out_ffn(l, hs, y_ssd, y_pool, y_attn, w_out_b, norm_ffn3, w_gu_b, w_down_b, tm_s, tf)
        state_outputs(k, v, ncx, ncbc, hout, nbuf, bs, ls, outs_s)

    stacked_p = [jnp.stack(s, axis=0) for s in outs_p]
    stacked_s = [jnp.stack(s, axis=0) for s in outs_s]
    return (hp.reshape(bp, lp, d), hs.reshape(bs, ls, d), *stacked_p, *stacked_s)
```

```python
import functools
import math

import jax
import jax.numpy as jnp
from jax import lax
from jax.experimental import pallas as pl
from jax.experimental.pallas import tpu as pltpu

F32 = jnp.float32
BF16 = jnp.bfloat16

RMS_EPS = 1e-6
ROPE_THETA = 10000.0
HEAD_DIM = 64
GROUP_WIDTH = 512
CONV_TAPS = 4
POOL_WINDOWS = (2, 4, 8, 16)
POOL_BUF = 15
SSD_CHUNK = 128
SSD_HEADS = 8
SSD_GROUPS = 2
ATTN_HEADS = 4
PAGE_SIZE = 128
NEG_BIG = -1e30
QK_SCALE_LOG2 = (HEAD_DIM ** -0.5) * math.log2(math.e)
SAFE_SCORE_BOUND = 64.0

VMEM_LIMIT_BYTES = 56 * 1024 * 1024

OFF_Z, OFF_XS, OFF_POOL, OFF_Q, OFF_K, OFF_V, OFF_BC, OFF_DT = 0, 512, 1024, 1536, 2048, 2560, 3072, 3328
IN_PROJ_PADDED = 3456


def _params(*semantics):
    return pltpu.CompilerParams(dimension_semantics=semantics, vmem_limit_bytes=VMEM_LIMIT_BYTES)


def _silu(x):
    return x / (1.0 + jnp.exp(-x))


def _dot(a, b):
    return jnp.dot(a, b, preferred_element_type=F32)


def _dot_nt(a, b):
    return lax.dot_general(a, b, (((1,), (1,)), ((), ())), preferred_element_type=F32)


def _dot_tn(a, b):
    return lax.dot_general(a, b, (((0,), (0,)), ((), ())), preferred_element_type=F32)


def _split3(x):
    hi = x.astype(BF16)
    r1 = x - hi.astype(F32)
    mid = r1.astype(BF16)
    lo = (r1 - mid.astype(F32)).astype(BF16)
    return hi, mid, lo


def _in_proj_kernel(x_ref, g_ref, w_ref, cos_ref, sin_ref, qg_ref, kg_ref, seg_ref,
                    z_ref, xs_ref, pool_ref, qb_ref, k_ref, v_ref, kb_ref, vb_ref, bc_ref, dt_ref, *, k_transposed):
    x = x_ref[...]
    ms = jnp.mean(x * x, axis=-1, keepdims=True)
    u = (x * lax.rsqrt(ms + RMS_EPS) * g_ref[...]).astype(BF16)

    def proj(off, width):
        return _dot_nt(u, w_ref[off:off + width, :])

    z_ref[...] = proj(OFF_Z, GROUP_WIDTH)
    xs_ref[...] = proj(OFF_XS, GROUP_WIDTH)
    pool_ref[...] = proj(OFF_POOL, GROUP_WIDTH)
    v = proj(OFF_V, GROUP_WIDTH)
    for h in range(ATTN_HEADS):
        v_ref[pl.ds(h, v.shape[0], stride=ATTN_HEADS), :] = v[:, h * 128:(h + 1) * 128]
    vb_ref[...] = v.astype(BF16)
    bc_ref[...] = proj(OFF_BC, 256)
    dt_ref[...] = proj(OFF_DT, 128)

    cos = cos_ref[...]
    sin = sin_ref[...]
    seg = seg_ref[...]
    lane = lax.broadcasted_iota(jnp.int32, (1, 128), 1)
    first_half = (lane % HEAD_DIM) < (HEAD_DIM // 2)

    def norm_rope(off, gain_ref, out_ref, out_b_ref, scale):
        gain = gain_ref[...]
        for s in range(GROUP_WIDTH // 256):
            y = proj(off + s * 256, 256)
            seg_ms = _dot((y * y).astype(BF16), seg)
            yn_wide = y * lax.rsqrt(seg_ms + RMS_EPS) * gain
            for t in range(2):
                lo = s * 256 + t * 128
                yn = yn_wide[:, t * 128:(t + 1) * 128]
                rot = jnp.where(first_half, -pltpu.roll(yn, 128 - HEAD_DIM // 2, 1),
                                pltpu.roll(yn, HEAD_DIM // 2, 1))
                roped = yn * cos + rot * sin
                if out_ref is not None and k_transposed:
                    out_ref[lo:lo + 128, :] = roped.T
                elif out_ref is not None:
                    out_ref[:, lo:lo + 128] = roped
                out_b_ref[:, lo:lo + 128] = (roped * scale).astype(BF16)

    norm_rope(OFF_Q, qg_ref, None, qb_ref, QK_SCALE_LOG2)
    norm_rope(OFF_K, kg_ref, k_ref, kb_ref, 1.0)


V_OUT = 5


K_OUT = 4


def _in_proj(layer, x, norm_mix, w_in_p, cos_t, sin_t, q_gain, k_gain, seg_mat, tm, n_seq=None):
    n_tok = x.shape[0]
    n_pos_blocks = cos_t.shape[0] // tm
    row = lambda i: (i, 0)
    k_transposed = n_seq is not None

    def out_spec(o, w):
        if o == V_OUT:
            return pl.BlockSpec((tm * ATTN_HEADS, 128), row)
        if o == K_OUT and k_transposed:
            return pl.BlockSpec((None, w, tm), lambda i: (i // n_pos_blocks, 0, i % n_pos_blocks))
        return pl.BlockSpec((tm, w), row)

    def out_struct(o, w, dt):
        if o == V_OUT:
            return jax.ShapeDtypeStruct((n_tok * ATTN_HEADS, 128), dt)
        if o == K_OUT and k_transposed:
            return jax.ShapeDtypeStruct((n_seq, w, cos_t.shape[0]), dt)
        return jax.ShapeDtypeStruct((n_tok, w), dt)

    widths = (GROUP_WIDTH,) * 8 + (256, 128)
    dtypes = (F32, F32, F32, BF16, F32, F32, BF16, BF16, F32, F32)
    return pl.pallas_call(
        functools.partial(_in_proj_kernel, k_transposed=k_transposed),
        grid=(n_tok // tm,),
        in_specs=[
            pl.BlockSpec((tm, x.shape[1]), row),
            pl.BlockSpec((None, 1, x.shape[1]), lambda i: (layer, 0, 0)),
            pl.BlockSpec((None, IN_PROJ_PADDED, x.shape[1]), lambda i: (layer, 0, 0)),
            pl.BlockSpec((tm, 128), lambda i: (i % n_pos_blocks, 0)),
            pl.BlockSpec((tm, 128), lambda i: (i % n_pos_blocks, 0)),
            pl.BlockSpec((None, 1, 256), lambda i: (layer, 0, 0)),
            pl.BlockSpec((None, 1, 256), lambda i: (layer, 0, 0)),
            pl.BlockSpec((256, 256), lambda i: (0, 0)),
        ],
        out_specs=[out_spec(o, w) for o, w in enumerate(widths)],
        out_shape=[out_struct(o, w, dt) for o, (w, dt) in enumerate(zip(widths, dtypes))],
        compiler_params=_params("parallel"),
        name="in_proj",
    )(x, norm_mix, w_in_p, cos_t, sin_t, q_gain, k_gain, seg_mat)


def _ssd_kernel(z_ref, xs_ref, bc_ref, dt_ref, cbx_ref, cbbc_ref, h0_ref,
                cwx_ref, cwbc_ref, cbiasx_ref, cbiasbc_ref, dtb_ref, alog_ref, dskip_ref, gain_ref,
                y_ref, ncx_ref, ncbc_ref, hout_ref,
                extx_ref, extbc_ref, h_ref, *, valid):
    Q = SSD_CHUNK
    c = pl.program_id(1)

    @pl.when(c == 0)
    def _():
        extx_ref[0:8, :] = cbx_ref[...]
        extbc_ref[0:8, :] = cbbc_ref[...]
        h_ref[...] = h0_ref[...]
        if valid < Q:
            extx_ref[8 + valid:8 + Q, :] = jnp.zeros((Q - valid, GROUP_WIDTH), F32)
            extbc_ref[8 + valid:8 + Q, :] = jnp.zeros((Q - valid, 256), F32)

    extx_ref[8:8 + valid, :] = xs_ref[...]
    extbc_ref[8:8 + valid, :] = bc_ref[...]

    def conv(ext_ref, w_ref, b_ref):
        acc = b_ref[...] + ext_ref[8:8 + Q, :] * w_ref[CONV_TAPS - 1:CONV_TAPS, :]
        for i in range(CONV_TAPS - 1):
            lo = 8 - (CONV_TAPS - 1) + i
            acc = acc + ext_ref[lo:lo + Q, :] * w_ref[i:i + 1, :]
        return _silu(acc)

    xs = conv(extx_ref, cwx_ref, cbiasx_ref)
    bc = conv(extbc_ref, cwbc_ref, cbiasbc_ref)
    tailx = extx_ref[valid:valid + 8, :]
    tailbc = extbc_ref[valid:valid + 8, :]
    extx_ref[0:8, :] = tailx
    extbc_ref[0:8, :] = tailbc
    ncx_ref[...] = tailx
    ncbc_ref[...] = tailbc

    dtr = dt_ref[...] + dtb_ref[...]
    dt = jnp.maximum(dtr, 0.0) + jnp.log1p(jnp.exp(-jnp.abs(dtr)))
    if valid < Q:
        dt = jnp.concatenate([dt, jnp.zeros((Q - valid, 128), F32)], axis=0)
    a = -jnp.exp(alog_ref[...])
    d_a = dt * a
    ri = lax.broadcasted_iota(jnp.int32, (Q, Q), 0)
    ci = lax.broadcasted_iota(jnp.int32, (Q, Q), 1)
    causal = ri >= ci
    tri = causal.astype(BF16)
    hi, mid, lo = _split3(d_a)
    cum = _dot(tri, hi) + _dot(tri, mid) + _dot(tri, lo)
    cum_t = cum.T
    dt_t = dt.T
    cum_last = cum[Q - 1:Q, :]
    e_cum = jnp.exp(cum)
    to_end = jnp.exp(cum_last - cum) * dt
    e_last = jnp.exp(cum_last)

    xs_b = xs.astype(BF16)
    h_prev = h_ref[...]
    h_prev_b = h_prev.astype(BF16)
    heads_per_group = SSD_HEADS // SSD_GROUPS
    gw = heads_per_group * HEAD_DIM
    y_parts = []
    xw_parts = []
    dec_parts = []
    for g in range(SSD_GROUPS):
        b_g = bc[:, g * HEAD_DIM:(g + 1) * HEAD_DIM].astype(BF16)
        c_g = bc[:, 128 + g * HEAD_DIM:128 + (g + 1) * HEAD_DIM].astype(BF16)
        cb = _dot_nt(c_g, b_g)
        y_off_g = _dot(c_g, h_prev_b[:, g * gw:(g + 1) * gw])
        for hh in range(heads_per_group):
            h = g * heads_per_group + hh
            seg = cum[:, h:h + 1] - cum_t[h:h + 1, :]
            decay = jnp.exp(jnp.where(causal, seg, NEG_BIG))
            scores = (cb * decay * dt_t[h:h + 1, :]).astype(BF16)
            x_h = xs_b[:, h * HEAD_DIM:(h + 1) * HEAD_DIM]
            y_h = _dot(scores, x_h)
            y_h = y_h + y_off_g[:, hh * HEAD_DIM:(hh + 1) * HEAD_DIM] * e_cum[:, h:h + 1]
            y_h = y_h + dskip_ref[:, h * HEAD_DIM:(h + 1) * HEAD_DIM] * xs[:, h * HEAD_DIM:(h + 1) * HEAD_DIM]
            y_parts.append(y_h)
            xw_parts.append(xs[:, h * HEAD_DIM:(h + 1) * HEAD_DIM] * to_end[:, h:h + 1])
            dec_parts.append(jnp.broadcast_to(e_last[:, h:h + 1], (1, HEAD_DIM)))
        xw_g = jnp.concatenate(xw_parts[-heads_per_group:], axis=1).astype(BF16)
        st_g = _dot_tn(b_g, xw_g)
        dec_g = jnp.concatenate(dec_parts[-heads_per_group:], axis=1)
        h_ref[:, g * gw:(g + 1) * gw] = dec_g * h_prev[:, g * gw:(g + 1) * gw] + st_g
    hout_ref[...] = h_ref[...]

    y = jnp.concatenate(y_parts, axis=1)[0:valid, :]
    y = y * _silu(z_ref[...])
    ms = jnp.mean(y * y, axis=-1, keepdims=True)
    y_ref[...] = y * lax.rsqrt(ms + RMS_EPS) * gain_ref[...]


def _ssd(layer, z, xs, bc, dt, cbx, cbbc, h0t, cwx, cwbc, cbiasx, cbiasbc, dtb, alog, dskip, gain,
         n_seq, seq_len):
    valid = min(seq_len, SSD_CHUNK)
    n_chunks = seq_len // valid
    tok = lambda b, c: (b * n_chunks + c, 0)
    seq3 = lambda b, c: (b, 0, 0)
    lay3 = lambda b, c: (layer, 0, 0)
    n_tok = n_seq * seq_len
    kern = functools.partial(_ssd_kernel, valid=valid)
    return pl.pallas_call(
        kern,
        grid=(n_seq, n_chunks),
        in_specs=[
            pl.BlockSpec((valid, GROUP_WIDTH), tok),
            pl.BlockSpec((valid, GROUP_WIDTH), tok),
            pl.BlockSpec((valid, 256), tok),
            pl.BlockSpec((valid, 128), tok),
            pl.BlockSpec((None, 8, GROUP_WIDTH), seq3),
            pl.BlockSpec((None, 8, 256), seq3),
            pl.BlockSpec((None, HEAD_DIM, GROUP_WIDTH), seq3),
            pl.BlockSpec((None, 8, GROUP_WIDTH), lay3),
            pl.BlockSpec((None, 8, 256), lay3),
            pl.BlockSpec((None, 1, GROUP_WIDTH), lay3),
            pl.BlockSpec((None, 1, 256), lay3),
            pl.BlockSpec((None, 1, 128), lay3),
            pl.BlockSpec((None, 1, 128), lay3),
            pl.BlockSpec((None, 1, GROUP_WIDTH), lay3),
            pl.BlockSpec((None, 1, GROUP_WIDTH), lay3),
        ],
        out_specs=[
            pl.BlockSpec((valid, GROUP_WIDTH), tok),
            pl.BlockSpec((None, 8, GROUP_WIDTH), seq3),
            pl.BlockSpec((None, 8, 256), seq3),
            pl.BlockSpec((None, HEAD_DIM, GROUP_WIDTH), seq3),
        ],
        out_shape=[
            jax.ShapeDtypeStruct((n_tok, GROUP_WIDTH), F32),
            jax.ShapeDtypeStruct((n_seq, 8, GROUP_WIDTH), F32),
            jax.ShapeDtypeStruct((n_seq, 8, 256), F32),
            jax.ShapeDtypeStruct((n_seq, HEAD_DIM, GROUP_WIDTH), F32),
        ],
        scratch_shapes=[
            pltpu.VMEM((8 + SSD_CHUNK, GROUP_WIDTH), F32),
            pltpu.VMEM((8 + SSD_CHUNK, 256), F32),
            pltpu.VMEM((HEAD_DIM, GROUP_WIDTH), F32),
        ],
        compiler_params=_params("parallel", "arbitrary"),
        name="ssd",
    )(z, xs, bc, dt, cbx, cbbc, h0t, cwx, cwbc, cbiasx, cbiasbc, dtb, alog, dskip, gain)


def _pool_kernel(x_ref, buf_ref, w_ref, scale_ref, y_ref, nbuf_ref, ext_ref, *, tl, pos0):
    c = pl.program_id(1)

    @pl.when(c == 0)
    def _():
        ext_ref[0:16, :] = buf_ref[...]

    ext_ref[16:16 + tl, :] = x_ref[...]
    row = lax.broadcasted_iota(jnp.int32, (tl, 1), 0)
    pos = pos0 + c * tl + row
    for g, w in enumerate(POOL_WINDOWS):
        sl = slice(g * 128, (g + 1) * 128)
        cur = ext_ref[16:16 + tl, sl]
        win = cur
        for k in range(1, w):
            win = win + ext_ref[16 - k:16 - k + tl, sl]
        cnt = jnp.minimum(w, pos + 1).astype(F32)
        pooled = (win / cnt - cur).astype(BF16)
        y_ref[:, sl] = _dot(pooled, w_ref[g]) * scale_ref[:, sl]
    tail = ext_ref[tl:tl + 16, :]
    ext_ref[0:16, :] = tail
    nbuf_ref[...] = tail


def _pool(layer, x, buf, pool_w, pool_scale, n_seq, seq_len, pos0):
    tl = min(seq_len, 256)
    n_chunks = seq_len // tl
    tok = lambda b, c: (b * n_chunks + c, 0)
    kern = functools.partial(_pool_kernel, tl=tl, pos0=pos0)
    return pl.pallas_call(
        kern,
        grid=(n_seq, n_chunks),
        in_specs=[
            pl.BlockSpec((tl, GROUP_WIDTH), tok),
            pl.BlockSpec((None, 16, GROUP_WIDTH), lambda b, c: (b, 0, 0)),
            pl.BlockSpec((None, 4, 128, 128), lambda b, c: (layer, 0, 0, 0)),
            pl.BlockSpec((None, 1, GROUP_WIDTH), lambda b, c: (layer, 0, 0)),
        ],
        out_specs=[
            pl.BlockSpec((tl, GROUP_WIDTH), tok),
            pl.BlockSpec((None, 16, GROUP_WIDTH), lambda b, c: (b, 0, 0)),
        ],
        out_shape=[
            jax.ShapeDtypeStruct((n_seq * seq_len, GROUP_WIDTH), F32),
            jax.ShapeDtypeStruct((n_seq, 16, GROUP_WIDTH), F32),
        ],
        scratch_shapes=[pltpu.VMEM((16 + tl, GROUP_WIDTH), F32)],
        compiler_params=_params("parallel", "arbitrary"),
        name="pool",
    )(x, buf, pool_w, pool_scale)


def _lam_value(lam_ref):
    lp = lam_ref[...]
    s1 = jnp.sum(lp[0:1, :] * lp[1:2, :], axis=-1, keepdims=True)
    s2 = jnp.sum(lp[2:3, :] * lp[3:4, :], axis=-1, keepdims=True)
    lam_init = lp[4:5, 0:1]
    return jnp.exp(s1) - jnp.exp(s2) + lam_init, lam_init


def _sub_norm(o, gain, lam_init):
    ms = jnp.mean(o * o, axis=-1, keepdims=True)
    return o * lax.rsqrt(ms + RMS_EPS) * gain * (1.0 - lam_init)


ATTN_BLOCK = 256


def _prompt_attn_kernel(flag_ref, q_ref, k_ref, v_ref, lam_ref, gain_ref, o_ref, m_ref, acc_ref, *, layer):
    T = ATTN_BLOCK
    i = pl.program_id(1)
    ones = jnp.ones((T, 128), BF16)

    def scores(j, hc, masked):
        start = pl.multiple_of(j * T, T)
        sl = slice(hc * HEAD_DIM, (hc + 1) * HEAD_DIM)
        s = _dot_nt(q_ref[:, sl], k_ref[pl.ds(start, T), sl])
        if masked:
            ri = lax.broadcasted_iota(jnp.int32, (T, T), 0)
            ci = lax.broadcasted_iota(jnp.int32, (T, T), 1)
            s = jnp.where(ci <= ri, s, NEG_BIG)
        return s

    m_ref[...] = jnp.zeros(m_ref.shape, F32)

    @pl.when(flag_ref[layer] == 0)
    def _():
        def fold_max(j, masked):
            for hc in range(2 * ATTN_HEADS):
                s = scores(j, hc, masked)
                m_ref[hc] = jnp.maximum(m_ref[hc], jnp.maximum(s[:, :128], s[:, 128:]))

        m_ref[...] = jnp.full(m_ref.shape, NEG_BIG, F32)
        fold_max(i, True)

        def body(j, carry):
            fold_max(j, False)
            return carry
        lax.fori_loop(0, i, body, 0)
        for hc in range(2 * ATTN_HEADS):
            m_ref[hc] = jnp.broadcast_to(jnp.max(m_ref[hc], axis=-1, keepdims=True), (T, 128))

    def accumulate(j, masked, first):
        start = pl.multiple_of(j * T, T)
        for h in range(ATTN_HEADS):
            v_aug = jnp.concatenate([v_ref[pl.ds(start, T), h * 128:(h + 1) * 128], ones], axis=1)
            for c in range(2):
                hc = 2 * h + c
                shift = jnp.tile(m_ref[hc], (1, T // 128))
                p = jnp.exp2(scores(j, hc, masked) - shift).astype(BF16)
                pv = _dot(p, v_aug)
                if first:
                    acc_ref[hc] = pv
                else:
                    acc_ref[hc] += pv

    accumulate(i, True, True)

    def body(t, carry):
        accumulate(2 * t, False, False)
        accumulate(2 * t + 1, False, False)
        return carry
    lax.fori_loop(0, i // 2, body, 0)

    @pl.when(i % 2 == 1)
    def _():
        accumulate(i - 1, False, False)

    lam, lam_init = _lam_value(lam_ref)
    gain = gain_ref[...]
    for h in range(ATTN_HEADS):
        a1 = acc_ref[2 * h]
        a2 = acc_ref[2 * h + 1]
        o = a1[:, :128] / a1[:, 128:] - lam * (a2[:, :128] / a2[:, 128:])
        o_ref[:, h * 128:(h + 1) * 128] = _sub_norm(o, gain, lam_init)


def _prompt_attn(layer, safe_flags, q, k, v, lam_p, subln, n_seq, seq_len):
    T = ATTN_BLOCK
    nq = seq_len // T
    kern = functools.partial(_prompt_attn_kernel, layer=layer)
    grid_spec = pltpu.PrefetchScalarGridSpec(
        num_scalar_prefetch=1,
        grid=(n_seq, nq),
        in_specs=[
            pl.BlockSpec((T, GROUP_WIDTH), lambda b, i, f: (b * nq + i, 0)),
            pl.BlockSpec((seq_len, GROUP_WIDTH), lambda b, i, f: (b, 0)),
            pl.BlockSpec((seq_len, GROUP_WIDTH), lambda b, i, f: (b, 0)),
            pl.BlockSpec((None, 8, 128), lambda b, i, f: (layer, 0, 0)),
            pl.BlockSpec((None, 1, 128), lambda b, i, f: (layer, 0, 0)),
        ],
        out_specs=pl.BlockSpec((T, GROUP_WIDTH), lambda b, i, f: (b * nq + i, 0)),
        scratch_shapes=[
            pltpu.VMEM((2 * ATTN_HEADS, T, 128), F32),
            pltpu.VMEM((2 * ATTN_HEADS, T, 256), F32),
        ],
    )
    return pl.pallas_call(
        kern,
        grid_spec=grid_spec,
        out_shape=jax.ShapeDtypeStruct((n_seq * seq_len, GROUP_WIDTH), F32),
        compiler_params=_params("parallel", "arbitrary"),
        name="prompt_attn",
    )(safe_flags, q, k, v, lam_p, subln)


PAGES_PER_STEP = 16


def _decode_attn_kernel(pt_ref, qrows_ref, *refs, n_new, n_steps):
    P = PAGES_PER_STEP
    k_refs = refs[0:P]
    v_refs = refs[P:2 * P]
    knew_ref, vnew_ref, lam_ref, gain_ref, o_ref, m_ref, l_ref, acc_ref = refs[2 * P:]
    j = pl.program_id(1)
    n_rows = 2 * ATTN_HEADS * n_new

    @pl.when(j == 0)
    def _():
        m_ref[...] = jnp.full(m_ref.shape, NEG_BIG, F32)
        l_ref[...] = jnp.zeros(l_ref.shape, F32)
        acc_ref[...] = jnp.zeros(acc_ref.shape, F32)

    qrows = qrows_ref[...]

    def head_values(v_ref, h):
        return v_ref[pl.ds(h, PAGE_SIZE, stride=ATTN_HEADS), :].astype(BF16)

    def update(s_list, v_ref_list):
        m_prev = m_ref[...]
        m_new = m_prev
        for s in s_list:
            m_new = jnp.maximum(m_new, jnp.max(s, axis=-1, keepdims=True))
        alpha = jnp.exp2(m_prev - m_new)
        l_new = alpha * l_ref[...]
        pv = [jnp.zeros((2 * n_new, 128), F32) for _ in range(ATTN_HEADS)]
        for s, v_ref in zip(s_list, v_ref_list):
            p = jnp.exp2(s - m_new)
            l_new = l_new + jnp.sum(p, axis=-1, keepdims=True)
            p_b = p.astype(BF16)
            for h in range(ATTN_HEADS):
                rows = slice(h * 2 * n_new, (h + 1) * 2 * n_new)
                pv[h] = pv[h] + _dot(p_b[rows, :], head_values(v_ref, h))
        for h in range(ATTN_HEADS):
            rows = slice(h * 2 * n_new, (h + 1) * 2 * n_new)
            acc_ref[rows, :] = alpha[rows, :] * acc_ref[rows, :] + pv[h]
        l_ref[...] = l_new
        m_ref[...] = m_new

    @pl.when(j < n_steps)
    def _():
        s_list = [_dot(qrows, k_refs[p][...].astype(BF16)) for p in range(P)]
        update(s_list, list(v_refs))

    @pl.when(j == n_steps)
    def _():
        s = _dot(qrows, knew_ref[...].astype(BF16))
        key = lax.broadcasted_iota(jnp.int32, (n_rows, PAGE_SIZE), 1)
        tok = lax.broadcasted_iota(jnp.int32, (n_rows, PAGE_SIZE), 0) % n_new
        s = jnp.where(key <= tok, s, NEG_BIG)
        update([s], [vnew_ref])
        lam, lam_init = _lam_value(lam_ref)
        gain = gain_ref[...]
        for h in range(ATTN_HEADS):
            r1 = slice(h * 2 * n_new, h * 2 * n_new + n_new)
            r2 = slice(h * 2 * n_new + n_new, (h + 1) * 2 * n_new)
            o = acc_ref[r1, :] / l_ref[r1, :] - lam * (acc_ref[r2, :] / l_ref[r2, :])
            o_ref[:, h * 128:(h + 1) * 128] = _sub_norm(o, gain, lam_init)


def _decode_attn(layer, page_table_flat, qrows, cache_kt, cache_v, knew_t, vnew, lam_p, subln,
                 n_seq, n_new, n_pages):
    P = PAGES_PER_STEP
    n_steps = n_pages // P
    n_rows = 2 * ATTN_HEADS * n_new

    def page_spec(p):
        def index(b, j, pt):
            step = jnp.minimum(j, n_steps - 1)
            return (layer, pt[b * n_pages + step * P + p], 0, 0)
        return pl.BlockSpec((None, None, GROUP_WIDTH, PAGE_SIZE), index)

    seq3 = lambda b, j, pt: (b, 0, 0)
    kern = functools.partial(_decode_attn_kernel, n_new=n_new, n_steps=n_steps)
    grid_spec = pltpu.PrefetchScalarGridSpec(
        num_scalar_prefetch=1,
        grid=(n_seq, n_steps + 1),
        in_specs=(
            [pl.BlockSpec((None, n_rows, GROUP_WIDTH), seq3)]
            + [page_spec(p) for p in range(P)]
            + [page_spec(p) for p in range(P)]
            + [pl.BlockSpec((None, GROUP_WIDTH, PAGE_SIZE), seq3),
               pl.BlockSpec((None, GROUP_WIDTH, PAGE_SIZE), seq3),
               pl.BlockSpec((None, 8, 128), lambda b, j, pt: (layer, 0, 0)),
               pl.BlockSpec((None, 1, 128), lambda b, j, pt: (layer, 0, 0))]
        ),
        out_specs=pl.BlockSpec((n_new, GROUP_WIDTH), lambda b, j, pt: (b, 0)),
        scratch_shapes=[
            pltpu.VMEM((n_rows, 128), F32),
            pltpu.VMEM((n_rows, 128), F32),
            pltpu.VMEM((n_rows, 128), F32),
        ],
    )
    return pl.pallas_call(
        kern,
        grid_spec=grid_spec,
        out_shape=jax.ShapeDtypeStruct((n_seq * n_new, GROUP_WIDTH), F32),
        compiler_params=_params("parallel", "arbitrary"),
        name="decode_attn",
    )(page_table_flat, qrows, *([cache_kt] * P), *([cache_v] * P), knew_t, vnew, lam_p, subln)


def _out_ffn_kernel(h_ref, ys_ref, yp_ref, ya_ref, wo_ref, g_ref, wg_ref, wu_ref, wd_ref,
                    o_ref, acc_ref, u_ref):
    f = pl.program_id(1)

    @pl.when(f == 0)
    def _():
        h1 = h_ref[...]
        h1 = h1 + _dot(ys_ref[...].astype(BF16), wo_ref[0:GROUP_WIDTH, :])
        h1 = h1 + _dot(yp_ref[...].astype(BF16), wo_ref[GROUP_WIDTH:2 * GROUP_WIDTH, :])
        h1 = h1 + _dot(ya_ref[...].astype(BF16), wo_ref[2 * GROUP_WIDTH:3 * GROUP_WIDTH, :])
        acc_ref[...] = h1
        ms = jnp.mean(h1 * h1, axis=-1, keepdims=True)
        u_ref[...] = (h1 * lax.rsqrt(ms + RMS_EPS) * g_ref[...]).astype(BF16)

    u = u_ref[...]
    act = (_silu(_dot(u, wg_ref[...])) * _dot(u, wu_ref[...])).astype(BF16)
    acc_ref[...] += _dot(act, wd_ref[...])

    @pl.when(f == pl.num_programs(1) - 1)
    def _():
        o_ref[...] = acc_ref[...]


def _out_ffn(layer, h, ys, yp, ya, w_out_b, norm_ffn, w_gu_b, w_down_b, tm, tf):
    n_tok, d = h.shape
    ffn = w_down_b.shape[1]
    nf = ffn // tf
    row = lambda i, f: (i, 0)
    return pl.pallas_call(
        _out_ffn_kernel,
        grid=(n_tok // tm, nf),
        in_specs=[
            pl.BlockSpec((tm, d), row),
            pl.BlockSpec((tm, GROUP_WIDTH), row),
            pl.BlockSpec((tm, GROUP_WIDTH), row),
            pl.BlockSpec((tm, GROUP_WIDTH), row),
            pl.BlockSpec((None, 3 * GROUP_WIDTH, d), lambda i, f: (layer, 0, 0)),
            pl.BlockSpec((None, 1, d), lambda i, f: (layer, 0, 0)),
            pl.BlockSpec((None, d, tf), lambda i, f: (layer, 0, f)),
            pl.BlockSpec((None, d, tf), lambda i, f: (layer, 0, nf + f)),
            pl.BlockSpec((None, tf, d), lambda i, f: (layer, f, 0)),
        ],
        out_specs=pl.BlockSpec((tm, d), row),
        out_shape=jax.ShapeDtypeStruct((n_tok, d), F32),
        scratch_shapes=[pltpu.VMEM((tm, d), F32), pltpu.VMEM((tm, d), BF16)],
        compiler_params=_params("parallel", "arbitrary"),
        name="out_ffn",
    )(h, ys, yp, ya, w_out_b, norm_ffn, w_gu_b, w_gu_b, w_down_b)


def _rope_tables(pos):
    half = HEAD_DIM // 2
    inv_freq = ROPE_THETA ** (-jnp.arange(half, dtype=F32) / half)
    ang = pos.astype(F32)[:, None] * inv_freq[None, :]
    return jnp.tile(jnp.cos(ang), (1, 128 // half)), jnp.tile(jnp.sin(ang), (1, 128 // half))


def _pad_rows_front(x, rows):
    pad = jnp.zeros(x.shape[:-2] + (rows - x.shape[-2], x.shape[-1]), x.dtype)
    return jnp.concatenate([pad, x], axis=-2)


def kernel(x_prompt, x_sample, cache_k, cache_v, state_conv, state_ssm, state_pool, page_table, norm_mix, w_in, conv_w, conv_b, dt_bias, a_log, d_skip, ssd_norm, pool_w, pool_scale, q_norm, k_norm, lam_q1, lam_k1, lam_q2, lam_k2, subln, w_out, norm_ffn, w_gate_up, w_down):
    depth = w_in.shape[0]
    bp, lp, d = x_prompt.shape
    bs, ls, _ = x_sample.shape
    n_pages = page_table.shape[1]
    past_len = n_pages * PAGE_SIZE
    ffn = w_down.shape[1]

    c0 = GROUP_WIDTH
    c1 = c0 + GROUP_WIDTH
    c2 = c1 + 256
    c3 = c2 + SSD_HEADS
    w_in_t = jnp.transpose(w_in, (0, 2, 1))
    w_in_p = jnp.concatenate(
        [w_in_t[:, 0:c1], w_in_t[:, c3:], w_in_t[:, c1:c2], w_in_t[:, c2:c3],
         jnp.zeros((depth, 128 - SSD_HEADS, w_in.shape[1]), w_in.dtype)], axis=1).astype(BF16)
    w_out_b = w_out.astype(BF16)
    w_gu_b = w_gate_up.astype(BF16)
    w_down_b = w_down.astype(BF16)
    pool_w_b = pool_w.astype(BF16)
    norm_mix3 = norm_mix[:, None, :]
    norm_ffn3 = norm_ffn[:, None, :]
    q_gain = jnp.tile(q_norm, (1, 4))[:, None, :]
    k_gain = jnp.tile(k_norm, (1, 4))[:, None, :]
    blk = jnp.arange(256) // HEAD_DIM
    seg_mat = ((blk[:, None] == blk[None, :]).astype(F32) / HEAD_DIM).astype(BF16)
    cw = jnp.concatenate([conv_w, jnp.zeros((depth, 8 - CONV_TAPS, conv_w.shape[-1]), F32)], axis=1)
    cwx, cwbc = cw[..., :GROUP_WIDTH], cw[..., GROUP_WIDTH:]
    cbiasx = conv_b[:, None, :GROUP_WIDTH]
    cbiasbc = conv_b[:, None, GROUP_WIDTH:]
    pad_heads = lambda x: jnp.concatenate([x, jnp.zeros((depth, 128 - SSD_HEADS), x.dtype)], axis=-1)[:, None, :]
    dtb = pad_heads(dt_bias)
    alog = pad_heads(a_log)
    dskip = jnp.repeat(d_skip, HEAD_DIM, axis=-1)[:, None, :]
    ssd_gain = ssd_norm[:, None, :]
    pool_scale3 = pool_scale[:, None, :]
    subln3 = subln[:, None, :]
    lam_inits = jnp.asarray([0.8 - 0.6 * math.exp(-0.3 * l) for l in range(depth)], F32)
    lam_rows = jnp.stack([lam_q1, lam_k1, lam_q2, lam_k2], axis=1)
    lam_rows = jnp.concatenate([lam_rows, jnp.zeros((depth, 4, 128 - HEAD_DIM), F32)], axis=-1)
    lam_p = jnp.concatenate(
        [lam_rows, jnp.broadcast_to(lam_inits[:, None, None], (depth, 1, 128)),
         jnp.zeros((depth, 3, 128), F32)], axis=1)

    score_bound = HEAD_DIM * QK_SCALE_LOG2 * jnp.max(jnp.abs(q_norm), axis=-1) * jnp.max(jnp.abs(k_norm), axis=-1)
    safe_flags = (score_bound < SAFE_SCORE_BOUND).astype(jnp.int32)

    cos_p, sin_p = _rope_tables(jnp.arange(lp, dtype=jnp.int32))
    pos_s = past_len + jnp.arange(ls, dtype=jnp.int32)
    cos_s, sin_s = _rope_tables(jnp.tile(pos_s, bs))

    zc = jnp.zeros((bp, 8, GROUP_WIDTH + 256), F32)
    cb_s = _pad_rows_front(state_conv, 8)
    h0_p = jnp.zeros((bp, HEAD_DIM, GROUP_WIDTH), F32)
    h0_s = jnp.transpose(state_ssm, (0, 1, 4, 2, 3)).reshape(depth, bs, HEAD_DIM, GROUP_WIDTH)
    pb_p = jnp.zeros((bp, 16, GROUP_WIDTH), F32)
    pb_s = _pad_rows_front(state_pool, 16)
    page_flat = page_table.reshape(-1).astype(jnp.int32)
    n_pool = cache_k.shape[1]
    cache_kt = jnp.transpose(cache_k, (0, 1, 3, 4, 5, 2)).reshape(depth, n_pool, GROUP_WIDTH, PAGE_SIZE)
    cache_v2 = cache_v.reshape(depth, n_pool, PAGE_SIZE * ATTN_HEADS, 2 * HEAD_DIM)
    eye_hc = jnp.eye(2 * ATTN_HEADS, dtype=BF16)

    hp = x_prompt.reshape(bp * lp, d)
    hs = x_sample.reshape(bs * ls, d)
    tm_p = 512 if (bp * lp) % 512 == 0 and lp % 512 == 0 else lp
    tm_s = bs * ls
    tf = ffn // 4 if (ffn // 4) % 128 == 0 else ffn

    outs_p = [[], [], [], [], []]
    outs_s = [[], [], [], [], []]

    def state_outputs(k, v, ncx, ncbc, hout, nbuf, n_seq, seq_len, store):
        store[0].append(k.reshape(n_seq, seq_len, ATTN_HEADS, 2, HEAD_DIM))
        store[1].append(v.reshape(n_seq, seq_len, ATTN_HEADS, 2 * HEAD_DIM))
        store[2].append(jnp.concatenate([ncx, ncbc], axis=-1)[:, 8 - (CONV_TAPS - 1):, :])
        store[3].append(jnp.transpose(hout.reshape(n_seq, HEAD_DIM, SSD_HEADS, HEAD_DIM), (0, 2, 3, 1)))
        store[4].append(nbuf[:, 16 - POOL_BUF:, :])

    for l in range(depth):
        ssd_w = (cwx, cwbc, cbiasx, cbiasbc, dtb, alog, dskip, ssd_gain)
        z, xs, pin, qb, k, v, kb, vb, bc, dt = _in_proj(l, hp, norm_mix3, w_in_p, cos_p, sin_p, q_gain, k_gain,
                                                        seg_mat, tm_p, n_seq=bp)
        k = jnp.transpose(k.reshape(bp, ATTN_HEADS, 2, HEAD_DIM, lp), (0, 4, 1, 2, 3))
        y_ssd, ncx, ncbc, hout = _ssd(l, z, xs, bc, dt, zc[..., :GROUP_WIDTH], zc[..., GROUP_WIDTH:], h0_p,
                                      *ssd_w, n_seq=bp, seq_len=lp)
        y_pool, nbuf = _pool(l, pin, pb_p, pool_w_b, pool_scale3, bp, lp, 0)
        y_attn = _prompt_attn(l, safe_flags, qb, kb, vb, lam_p, subln3, bp, lp)
        hp = _out_ffn(l, hp, y_ssd, y_pool, y_attn, w_out_b, norm_ffn3, w_gu_b, w_down_b, tm_p, tf)
        state_outputs(k, v, ncx, ncbc, hout, nbuf, bp, lp, outs_p)

        z, xs, pin, qb, k, v, kb, vb, bc, dt = _in_proj(l, hs, norm_mix3, w_in_p, cos_s, sin_s, q_gain, k_gain,
                                                        seg_mat, tm_s)
        y_ssd, ncx, ncbc, hout = _ssd(l, z, xs, bc, dt, cb_s[l, ..., :GROUP_WIDTH], cb_s[l, ..., GROUP_WIDTH:],
                                      h0_s[l], *ssd_w, n_seq=bs, seq_len=ls)
        y_pool, nbuf = _pool(l, pin, pb_s[l], pool_w_b, pool_scale3, bs, ls, past_len)
        q4 = qb.reshape(bs, ls, 2 * ATTN_HEADS, HEAD_DIM)
        qrows = jnp.transpose(q4[:, :, :, None, :] * eye_hc[None, None, :, :, None], (0, 2, 1, 3, 4))
        qrows = qrows.reshape(bs, 2 * ATTN_HEADS * ls, GROUP_WIDTH)
        knew_t = jnp.transpose(k.reshape(bs, ls, GROUP_WIDTH), (0, 2, 1))
        knew_t = jnp.concatenate([knew_t, jnp.zeros((bs, GROUP_WIDTH, PAGE_SIZE - ls), F32)], axis=2)
        vnew = v.reshape(bs, ls * ATTN_HEADS, 2 * HEAD_DIM)
        vnew = jnp.concatenate([vnew, jnp.zeros((bs, (PAGE_SIZE - ls) * ATTN_HEADS, 2 * HEAD_DIM), F32)], axis=1)
        y_attn = _decode_attn(l, page_flat, qrows, cache_kt, cache_v2, knew_t, vnew, lam_p, subln3,
                              bs, ls, n_pages)
        hs = _out_ffn(l, hs, y_ssd, y_pool, y_attn, w_out_b, norm_ffn3, w_gu_b, w_down_b, tm_s, tf)
        state_outputs(k, v, ncx, ncbc, hout, nbuf, bs, ls, outs_s)

    stacked_p = [jnp.stack(s, axis=0) for s in outs_p]
    stacked_s = [jnp.stack(s, axis=0) for s in outs_s]
    return (hp.reshape(bp, lp, d), hs.reshape(bs, ls, d), *stacked_p, *stacked_s)
```

```python
import functools
import math

import jax
import jax.numpy as jnp
from jax import lax
from jax.experimental import pallas as pl
from jax.experimental.pallas import tpu as pltpu

F32 = jnp.float32
BF16 = jnp.bfloat16

RMS_EPS = 1e-6
ROPE_THETA = 10000.0
HEAD_DIM = 64
GROUP_WIDTH = 512
CONV_TAPS = 4
POOL_WINDOWS = (2, 4, 8, 16)
POOL_BUF = 15
SSD_CHUNK = 128
SSD_HEADS = 8
SSD_GROUPS = 2
ATTN_HEADS = 4
PAGE_SIZE = 128
NEG_BIG = -1e30
QK_SCALE_LOG2 = (HEAD_DIM ** -0.5) * math.log2(math.e)
SAFE_SCORE_BOUND = 64.0

VMEM_LIMIT_BYTES = 56 * 1024 * 1024

OFF_Z, OFF_XS, OFF_POOL, OFF_Q, OFF_K, OFF_V, OFF_BC, OFF_DT = 0, 512, 1024, 1536, 2048, 2560, 3072, 3328
IN_PROJ_PADDED = 3456


def _params(*semantics):
    return pltpu.CompilerParams(dimension_semantics=semantics, vmem_limit_bytes=VMEM_LIMIT_BYTES)


def _silu(x):
    return x / (1.0 + jnp.exp(-x))


def _dot(a, b):
    return jnp.dot(a, b, preferred_element_type=F32)


def _dot_nt(a, b):
    return lax.dot_general(a, b, (((1,), (1,)), ((), ())), preferred_element_type=F32)


def _dot_tn(a, b):
    return lax.dot_general(a, b, (((0,), (0,)), ((), ())), preferred_element_type=F32)


def _split3(x):
    hi = x.astype(BF16)
    r1 = x - hi.astype(F32)
    mid = r1.astype(BF16)
    lo = (r1 - mid.astype(F32)).astype(BF16)
    return hi, mid, lo


def _in_proj_kernel(x_ref, g_ref, w_ref, cos_ref, sin_ref, qg_ref, kg_ref, seg_ref,
                    z_ref, xs_ref, pool_ref, qb_ref, k_ref, v_ref, kb_ref, vb_ref, bc_ref, dt_ref, *, k_transposed):
    x = x_ref[...]
    ms = jnp.mean(x * x, axis=-1, keepdims=True)
    u = (x * lax.rsqrt(ms + RMS_EPS) * g_ref[...]).astype(BF16)

    def proj(off, width):
        return _dot_nt(u, w_ref[off:off + width, :])

    z_ref[...] = proj(OFF_Z, GROUP_WIDTH)
    xs_ref[...] = proj(OFF_XS, GROUP_WIDTH)
    pool_ref[...] = proj(OFF_POOL, GROUP_WIDTH)
    v = proj(OFF_V, GROUP_WIDTH)
    for h in range(ATTN_HEADS):
        v_ref[pl.ds(h, v.shape[0], stride=ATTN_HEADS), :] = v[:, h * 128:(h + 1) * 128]
    vb_ref[...] = v.astype(BF16)
    bc_ref[...] = proj(OFF_BC, 256)
    dt_ref[...] = proj(OFF_DT, 128)

    cos = cos_ref[...]
    sin = sin_ref[...]
    seg = seg_ref[...]
    lane = lax.broadcasted_iota(jnp.int32, (1, 128), 1)
    first_half = (lane % HEAD_DIM) < (HEAD_DIM // 2)

    def norm_rope(off, gain_ref, out_ref, out_b_ref, scale):
        gain = gain_ref[...]
        for s in range(GROUP_WIDTH // 256):
            y = proj(off + s * 256, 256)
            seg_ms = _dot((y * y).astype(BF16), seg)
            yn_wide = y * lax.rsqrt(seg_ms + RMS_EPS) * gain
            for t in range(2):
                lo = s * 256 + t * 128
                yn = yn_wide[:, t * 128:(t + 1) * 128]
                rot = jnp.where(first_half, -pltpu.roll(yn, 128 - HEAD_DIM // 2, 1),
                                pltpu.roll(yn, HEAD_DIM // 2, 1))
                roped = yn * cos + rot * sin
                if out_ref is not None and k_transposed:
                    out_ref[lo:lo + 128, :] = roped.T
                elif out_ref is not None:
                    out_ref[:, lo:lo + 128] = roped
                out_b_ref[:, lo:lo + 128] = (roped * scale).astype(BF16)

    norm_rope(OFF_Q, qg_ref, None, qb_ref, QK_SCALE_LOG2)
    norm_rope(OFF_K, kg_ref, k_ref, kb_ref, 1.0)


V_OUT = 5


K_OUT = 4


def _in_proj(layer, x, norm_mix, w_in_p, cos_t, sin_t, q_gain, k_gain, seg_mat, tm, n_seq=None):
    n_tok = x.shape[0]
    n_pos_blocks = cos_t.shape[0] // tm
    row = lambda i: (i, 0)
    k_transposed = n_seq is not None

    def out_spec(o, w):
        if o == V_OUT:
            return pl.BlockSpec((tm * ATTN_HEADS, 128), row)
        if o == K_OUT and k_transposed:
            return pl.BlockSpec((None, w, tm), lambda i: (i // n_pos_blocks, 0, i % n_pos_blocks))
        return pl.BlockSpec((tm, w), row)

    def out_struct(o, w, dt):
        if o == V_OUT:
            return jax.ShapeDtypeStruct((n_tok * ATTN_HEADS, 128), dt)
        if o == K_OUT and k_transposed:
            return jax.ShapeDtypeStruct((n_seq, w, cos_t.shape[0]), dt)
        return jax.ShapeDtypeStruct((n_tok, w), dt)

    widths = (GROUP_WIDTH,) * 8 + (256, 128)
    dtypes = (F32, F32, F32, BF16, F32, F32, BF16, BF16, F32, F32)
    return pl.pallas_call(
        functools.partial(_in_proj_kernel, k_transposed=k_transposed),
        grid=(n_tok // tm,),
        in_specs=[
            pl.BlockSpec((tm, x.shape[1]), row),
            pl.BlockSpec((None, 1, x.shape[1]), lambda i: (layer, 0, 0)),
            pl.BlockSpec((None, IN_PROJ_PADDED, x.shape[1]), lambda i: (layer, 0, 0)),
            pl.BlockSpec((tm, 128), lambda i: (i % n_pos_blocks, 0)),
            pl.BlockSpec((tm, 128), lambda i: (i % n_pos_blocks, 0)),
            pl.BlockSpec((None, 1, 256), lambda i: (layer, 0, 0)),
            pl.BlockSpec((None, 1, 256), lambda i: (layer, 0, 0)),
            pl.BlockSpec((256, 256), lambda i: (0, 0)),
        ],
        out_specs=[out_spec(o, w) for o, w in enumerate(widths)],
        out_shape=[out_struct(o, w, dt) for o, (w, dt) in enumerate(zip(widths, dtypes))],
        compiler_params=_params("parallel"),
        name="in_proj",
    )(x, norm_mix, w_in_p, cos_t, sin_t, q_gain, k_gain, seg_mat)


def _ssd_kernel(z_ref, xs_ref, bc_ref, dt_ref, cbx_ref, cbbc_ref, h0_ref,
                cwx_ref, cwbc_ref, cbiasx_ref, cbiasbc_ref, dtb_ref, alog_ref, dskip_ref, gain_ref, expand_ref,
                y_ref, ncx_ref, ncbc_ref, hout_ref,
                extx_ref, extbc_ref, h_ref, *, valid):
    Q = SSD_CHUNK
    c = pl.program_id(1)

    @pl.when(c == 0)
    def _():
        extx_ref[0:8, :] = cbx_ref[...]
        extbc_ref[0:8, :] = cbbc_ref[...]
        h_ref[...] = h0_ref[...]
        if valid < Q:
            extx_ref[8 + valid:8 + Q, :] = jnp.zeros((Q - valid, GROUP_WIDTH), F32)
            extbc_ref[8 + valid:8 + Q, :] = jnp.zeros((Q - valid, 256), F32)

    extx_ref[8:8 + valid, :] = xs_ref[...]
    extbc_ref[8:8 + valid, :] = bc_ref[...]

    def conv(ext_ref, w_ref, b_ref):
        acc = b_ref[...] + ext_ref[8:8 + Q, :] * w_ref[CONV_TAPS - 1:CONV_TAPS, :]
        for i in range(CONV_TAPS - 1):
            lo = 8 - (CONV_TAPS - 1) + i
            acc = acc + ext_ref[lo:lo + Q, :] * w_ref[i:i + 1, :]
        return _silu(acc)

    xs = conv(extx_ref, cwx_ref, cbiasx_ref)
    bc = conv(extbc_ref, cwbc_ref, cbiasbc_ref)
    tailx = extx_ref[valid:valid + 8, :]
    tailbc = extbc_ref[valid:valid + 8, :]
    extx_ref[0:8, :] = tailx
    extbc_ref[0:8, :] = tailbc
    ncx_ref[...] = tailx
    ncbc_ref[...] = tailbc

    dtr = dt_ref[...] + dtb_ref[...]
    dt = jnp.maximum(dtr, 0.0) + jnp.log1p(jnp.exp(-jnp.abs(dtr)))
    if valid < Q:
        dt = jnp.concatenate([dt, jnp.zeros((Q - valid, 128), F32)], axis=0)
    a = -jnp.exp(alog_ref[...])
    d_a = dt * a
    ri = lax.broadcasted_iota(jnp.int32, (Q, Q), 0)
    ci = lax.broadcasted_iota(jnp.int32, (Q, Q), 1)
    causal = ri >= ci
    tri = causal.astype(BF16)
    hi, mid, lo = _split3(d_a)
    cum = _dot(tri, hi) + _dot(tri, mid) + _dot(tri, lo)
    cum_t = cum.T
    dt_t = dt.T
    cum_last = cum[Q - 1:Q, :]
    e_cum = jnp.exp(cum)
    to_end = jnp.exp(cum_last - cum) * dt
    e_last = jnp.exp(cum_last)

    xs_b = xs.astype(BF16)
    h_prev = h_ref[...]
    h_prev_b = h_prev.astype(BF16)
    heads_per_group = SSD_HEADS // SSD_GROUPS
    gw = heads_per_group * HEAD_DIM
    expand = expand_ref[...]
    e_cum_x = _dot(e_cum.astype(BF16), expand)
    to_end_x = _dot(to_end.astype(BF16), expand)
    xw = (xs * to_end_x).astype(BF16)
    low_half = lax.broadcasted_iota(jnp.int32, (1, 128), 1) < HEAD_DIM
    y_off_parts = []
    y_diag_parts = []
    for g in range(SSD_GROUPS):
        b_g = bc[:, g * HEAD_DIM:(g + 1) * HEAD_DIM].astype(BF16)
        c_g = bc[:, 128 + g * HEAD_DIM:128 + (g + 1) * HEAD_DIM].astype(BF16)
        cb = _dot_nt(c_g, b_g)
        y_off_parts.append(_dot(c_g, h_prev_b[:, g * gw:(g + 1) * gw]))
        score_parts = []
        dec_parts = []
        for hh in range(heads_per_group):
            h = g * heads_per_group + hh
            seg = cum[:, h:h + 1] - cum_t[h:h + 1, :]
            decay = jnp.exp(jnp.where(causal, seg, NEG_BIG))
            score_parts.append((cb * decay * dt_t[h:h + 1, :]).astype(BF16))
            dec_parts.append(jnp.broadcast_to(e_last[:, h:h + 1], (1, HEAD_DIM)))
        for pr in range(heads_per_group // 2):
            lo = (g * heads_per_group + 2 * pr) * HEAD_DIM
            x_pair = xs_b[:, lo:lo + 128]
            zero = jnp.zeros_like(x_pair)
            x_bd = jnp.concatenate([jnp.where(low_half, x_pair, zero), jnp.where(low_half, zero, x_pair)], axis=0)
            s_pair = jnp.concatenate(score_parts[2 * pr:2 * pr + 2], axis=1)
            y_diag_parts.append(_dot(s_pair, x_bd))
        st_g = _dot_tn(b_g, xw[:, g * gw:(g + 1) * gw])
        dec_g = jnp.concatenate(dec_parts, axis=1)
        h_ref[:, g * gw:(g + 1) * gw] = dec_g * h_prev[:, g * gw:(g + 1) * gw] + st_g
    hout_ref[...] = h_ref[...]

    y = (jnp.concatenate(y_diag_parts, axis=1) + jnp.concatenate(y_off_parts, axis=1) * e_cum_x
         + dskip_ref[...] * xs)[0:valid, :]
    y = y * _silu(z_ref[...])
    ms = jnp.mean(y * y, axis=-1, keepdims=True)
    y_ref[...] = y * lax.rsqrt(ms + RMS_EPS) * gain_ref[...]


def _ssd(layer, z, xs, bc, dt, cbx, cbbc, h0t, cwx, cwbc, cbiasx, cbiasbc, dtb, alog, dskip, gain, expand,
         n_seq, seq_len):
    valid = min(seq_len, SSD_CHUNK)
    n_chunks = seq_len // valid
    tok = lambda b, c: (b * n_chunks + c, 0)
    seq3 = lambda b, c: (b, 0, 0)
    lay3 = lambda b, c: (layer, 0, 0)
    n_tok = n_seq * seq_len
    kern = functools.partial(_ssd_kernel, valid=valid)
    return pl.pallas_call(
        kern,
        grid=(n_seq, n_chunks),
        in_specs=[
            pl.BlockSpec((valid, GROUP_WIDTH), tok),
            pl.BlockSpec((valid, GROUP_WIDTH), tok),
            pl.BlockSpec((valid, 256), tok),
            pl.BlockSpec((valid, 128), tok),
            pl.BlockSpec((None, 8, GROUP_WIDTH), seq3),
            pl.BlockSpec((None, 8, 256), seq3),
            pl.BlockSpec((None, HEAD_DIM, GROUP_WIDTH), seq3),
            pl.BlockSpec((None, 8, GROUP_WIDTH), lay3),
            pl.BlockSpec((None, 8, 256), lay3),
            pl.BlockSpec((None, 1, GROUP_WIDTH), lay3),
            pl.BlockSpec((None, 1, 256), lay3),
            pl.BlockSpec((None, 1, 128), lay3),
            pl.BlockSpec((None, 1, 128), lay3),
            pl.BlockSpec((None, 1, GROUP_WIDTH), lay3),
            pl.BlockSpec((None, 1, GROUP_WIDTH), lay3),
            pl.BlockSpec((128, GROUP_WIDTH), lambda b, c: (0, 0)),
        ],
        out_specs=[
            pl.BlockSpec((valid, GROUP_WIDTH), tok),
            pl.BlockSpec((None, 8, GROUP_WIDTH), seq3),
            pl.BlockSpec((None, 8, 256), seq3),
            pl.BlockSpec((None, HEAD_DIM, GROUP_WIDTH), seq3),
        ],
        out_shape=[
            jax.ShapeDtypeStruct((n_tok, GROUP_WIDTH), F32),
            jax.ShapeDtypeStruct((n_seq, 8, GROUP_WIDTH), F32),
            jax.ShapeDtypeStruct((n_seq, 8, 256), F32),
            jax.ShapeDtypeStruct((n_seq, HEAD_DIM, GROUP_WIDTH), F32),
        ],
        scratch_shapes=[
            pltpu.VMEM((8 + SSD_CHUNK, GROUP_WIDTH), F32),
            pltpu.VMEM((8 + SSD_CHUNK, 256), F32),
            pltpu.VMEM((HEAD_DIM, GROUP_WIDTH), F32),
        ],
        compiler_params=_params("parallel", "arbitrary"),
        name="ssd",
    )(z, xs, bc, dt, cbx, cbbc, h0t, cwx, cwbc, cbiasx, cbiasbc, dtb, alog, dskip, gain, expand)


def _pool_kernel(x_ref, buf_ref, w_ref, scale_ref, y_ref, nbuf_ref, ext_ref, *, tl, pos0):
    c = pl.program_id(1)

    @pl.when(c == 0)
    def _():
        ext_ref[0:16, :] = buf_ref[...]

    ext_ref[16:16 + tl, :] = x_ref[...]
    row = lax.broadcasted_iota(jnp.int32, (tl, 1), 0)
    pos = pos0 + c * tl + row
    for g, w in enumerate(POOL_WINDOWS):
        sl = slice(g * 128, (g + 1) * 128)
        cur = ext_ref[16:16 + tl, sl]
        win = cur
        for k in range(1, w):
            win = win + ext_ref[16 - k:16 - k + tl, sl]
        cnt = jnp.minimum(w, pos + 1).astype(F32)
        pooled = (win / cnt - cur).astype(BF16)
        y_ref[:, sl] = _dot(pooled, w_ref[g]) * scale_ref[:, sl]
    tail = ext_ref[tl:tl + 16, :]
    ext_ref[0:16, :] = tail
    nbuf_ref[...] = tail


def _pool(layer, x, buf, pool_w, pool_scale, n_seq, seq_len, pos0):
    tl = min(seq_len, 512)
    n_chunks = seq_len // tl
    tok = lambda b, c: (b * n_chunks + c, 0)
    kern = functools.partial(_pool_kernel, tl=tl, pos0=pos0)
    return pl.pallas_call(
        kern,
        grid=(n_seq, n_chunks),
        in_specs=[
            pl.BlockSpec((tl, GROUP_WIDTH), tok),
            pl.BlockSpec((None, 16, GROUP_WIDTH), lambda b, c: (b, 0, 0)),
            pl.BlockSpec((None, 4, 128, 128), lambda b, c: (layer, 0, 0, 0)),
            pl.BlockSpec((None, 1, GROUP_WIDTH), lambda b, c: (layer, 0, 0)),
        ],
        out_specs=[
            pl.BlockSpec((tl, GROUP_WIDTH), tok),
            pl.BlockSpec((None, 16, GROUP_WIDTH), lambda b, c: (b, 0, 0)),
        ],
        out_shape=[
            jax.ShapeDtypeStruct((n_seq * seq_len, GROUP_WIDTH), F32),
            jax.ShapeDtypeStruct((n_seq, 16, GROUP_WIDTH), F32),
        ],
        scratch_shapes=[pltpu.VMEM((16 + tl, GROUP_WIDTH), F32)],
        compiler_params=_params("parallel", "arbitrary"),
        name="pool",
    )(x, buf, pool_w, pool_scale)


def _lam_value(lam_ref):
    lp = lam_ref[...]
    s1 = jnp.sum(lp[0:1, :] * lp[1:2, :], axis=-1, keepdims=True)
    s2 = jnp.sum(lp[2:3, :] * lp[3:4, :], axis=-1, keepdims=True)
    lam_init = lp[4:5, 0:1]
    return jnp.exp(s1) - jnp.exp(s2) + lam_init, lam_init


def _sub_norm(o, gain, lam_init):
    ms = jnp.mean(o * o, axis=-1, keepdims=True)
    return o * lax.rsqrt(ms + RMS_EPS) * gain * (1.0 - lam_init)


ATTN_BLOCK = 256


def _prompt_attn_kernel(flag_ref, q_ref, k_ref, v_ref, lam_ref, gain_ref, o_ref, m_ref, acc_ref, *, layer):
    T = ATTN_BLOCK
    i = pl.program_id(1)
    ones = jnp.ones((T, 128), BF16)

    def scores(j, hc, masked):
        start = pl.multiple_of(j * T, T)
        sl = slice(hc * HEAD_DIM, (hc + 1) * HEAD_DIM)
        s = _dot_nt(q_ref[:, sl], k_ref[pl.ds(start, T), sl])
        if masked:
            ri = lax.broadcasted_iota(jnp.int32, (T, T), 0)
            ci = lax.broadcasted_iota(jnp.int32, (T, T), 1)
            s = jnp.where(ci <= ri, s, NEG_BIG)
        return s

    m_ref[...] = jnp.zeros(m_ref.shape, F32)

    @pl.when(flag_ref[layer] == 0)
    def _():
        def fold_max(j, masked):
            for hc in range(2 * ATTN_HEADS):
                s = scores(j, hc, masked)
                m_ref[hc] = jnp.maximum(m_ref[hc], jnp.maximum(s[:, :128], s[:, 128:]))

        m_ref[...] = jnp.full(m_ref.shape, NEG_BIG, F32)
        fold_max(i, True)

        def body(j, carry):
            fold_max(j, False)
            return carry
        lax.fori_loop(0, i, body, 0)
        for hc in range(2 * ATTN_HEADS):
            m_ref[hc] = jnp.broadcast_to(jnp.max(m_ref[hc], axis=-1, keepdims=True), (T, 128))

    def accumulate(j, masked, first):
        start = pl.multiple_of(j * T, T)
        for h in range(ATTN_HEADS):
            v_aug = jnp.concatenate([v_ref[pl.ds(start, T), h * 128:(h + 1) * 128], ones], axis=1)
            for c in range(2):
                hc = 2 * h + c
                shift = jnp.tile(m_ref[hc], (1, T // 128))
                p = jnp.exp2(scores(j, hc, masked) - shift).astype(BF16)
                pv = _dot(p, v_aug)
                if first:
                    acc_ref[hc] = pv
                else:
                    acc_ref[hc] += pv

    accumulate(i, True, True)

    def body(t, carry):
        accumulate(2 * t, False, False)
        accumulate(2 * t + 1, False, False)
        return carry
    lax.fori_loop(0, i // 2, body, 0)

    @pl.when(i % 2 == 1)
    def _():
        accumulate(i - 1, False, False)

    lam, lam_init = _lam_value(lam_ref)
    gain = gain_ref[...]
    for h in range(ATTN_HEADS):
        a1 = acc_ref[2 * h]
        a2 = acc_ref[2 * h + 1]
        o = a1[:, :128] / a1[:, 128:] - lam * (a2[:, :128] / a2[:, 128:])
        o_ref[:, h * 128:(h + 1) * 128] = _sub_norm(o, gain, lam_init)


def _prompt_attn(layer, safe_flags, q, k, v, lam_p, subln, n_seq, seq_len):
    T = ATTN_BLOCK
    nq = seq_len // T
    kern = functools.partial(_prompt_attn_kernel, layer=layer)
    grid_spec = pltpu.PrefetchScalarGridSpec(
        num_scalar_prefetch=1,
        grid=(n_seq, nq),
        in_specs=[
            pl.BlockSpec((T, GROUP_WIDTH), lambda b, i, f: (b * nq + i, 0)),
            pl.BlockSpec((seq_len, GROUP_WIDTH), lambda b, i, f: (b, 0)),
            pl.BlockSpec((seq_len, GROUP_WIDTH), lambda b, i, f: (b, 0)),
            pl.BlockSpec((None, 8, 128), lambda b, i, f: (layer, 0, 0)),
            pl.BlockSpec((None, 1, 128), lambda b, i, f: (layer, 0, 0)),
        ],
        out_specs=pl.BlockSpec((T, GROUP_WIDTH), lambda b, i, f: (b * nq + i, 0)),
        scratch_shapes=[
            pltpu.VMEM((2 * ATTN_HEADS, T, 128), F32),
            pltpu.VMEM((2 * ATTN_HEADS, T, 256), F32),
        ],
    )
    return pl.pallas_call(
        kern,
        grid_spec=grid_spec,
        out_shape=jax.ShapeDtypeStruct((n_seq * seq_len, GROUP_WIDTH), F32),
        compiler_params=_params("parallel", "arbitrary"),
        name="prompt_attn",
    )(safe_flags, q, k, v, lam_p, subln)


DECODE_PAGES_PER_ITER = 8
DECODE_SLOTS = 2 * DECODE_PAGES_PER_ITER


def _decode_attn_kernel(pt_ref, qrows_ref, knew_ref, vnew_ref, lam_ref, gain_ref, kt_hbm, v_hbm, o_ref,
                        kbuf, vbuf, sem, m_ref, l_ref, acc_ref, *, layer, n_new, n_iters, n_seq):
    P = DECODE_PAGES_PER_ITER
    b = pl.program_id(0)
    total_iters = n_seq * n_iters
    n_rows = 2 * ATTN_HEADS * n_new

    def page_copies(g):
        base = (g % 2) * P
        copies = []
        for s in range(P):
            page = pt_ref[g * P + s]
            copies.append(pltpu.make_async_copy(kt_hbm.at[layer, page], kbuf.at[base + s], sem.at[0, base + s]))
            copies.append(pltpu.make_async_copy(v_hbm.at[layer, page], vbuf.at[base + s], sem.at[1, base + s]))
        return copies

    def start_iteration(g):
        for c in page_copies(g):
            c.start()

    @pl.when(b == 0)
    def _():
        start_iteration(0)
        start_iteration(1)

    m_ref[...] = jnp.full(m_ref.shape, NEG_BIG, F32)
    l_ref[...] = jnp.zeros(l_ref.shape, F32)
    acc_ref[...] = jnp.zeros(acc_ref.shape, F32)
    qrows = qrows_ref[...]

    def head_values(v_ref, h):
        return v_ref[pl.ds(h, PAGE_SIZE, stride=ATTN_HEADS), :].astype(BF16)

    def update(s_list, v_ref_list):
        m_prev = m_ref[...]
        m_new = m_prev
        for s in s_list:
            m_new = jnp.maximum(m_new, jnp.max(s, axis=-1, keepdims=True))
        alpha = jnp.exp2(m_prev - m_new)
        l_new = alpha * l_ref[...]
        pv = [jnp.zeros((2 * n_new, 128), F32) for _ in range(ATTN_HEADS)]
        for s, v_ref in zip(s_list, v_ref_list):
            p = jnp.exp2(s - m_new)
            l_new = l_new + jnp.sum(p, axis=-1, keepdims=True)
            p_b = p.astype(BF16)
            for h in range(ATTN_HEADS):
                rows = slice(h * 2 * n_new, (h + 1) * 2 * n_new)
                pv[h] = pv[h] + _dot(p_b[rows, :], head_values(v_ref, h))
        for h in range(ATTN_HEADS):
            rows = slice(h * 2 * n_new, (h + 1) * 2 * n_new)
            acc_ref[rows, :] = alpha[rows, :] * acc_ref[rows, :] + pv[h]
        l_ref[...] = l_new
        m_ref[...] = m_new

    def iteration(t, carry):
        g = b * n_iters + t
        base = (g % 2) * P
        for c in page_copies(g):
            c.wait()
        s_list = [_dot(qrows, kbuf[base + s].astype(BF16)) for s in range(P)]
        update(s_list, [vbuf.at[base + s] for s in range(P)])

        @pl.when(g + 2 < total_iters)
        def _():
            start_iteration(g + 2)
        return carry

    lax.fori_loop(0, n_iters, iteration, 0)

    s = _dot(qrows, knew_ref[...].astype(BF16))
    key = lax.broadcasted_iota(jnp.int32, (n_rows, PAGE_SIZE), 1)
    tok = lax.broadcasted_iota(jnp.int32, (n_rows, PAGE_SIZE), 0) % n_new
    s = jnp.where(key <= tok, s, NEG_BIG)
    update([s], [vnew_ref])
    lam, lam_init = _lam_value(lam_ref)
    gain = gain_ref[...]
    for h in range(ATTN_HEADS):
        r1 = slice(h * 2 * n_new, h * 2 * n_new + n_new)
        r2 = slice(h * 2 * n_new + n_new, (h + 1) * 2 * n_new)
        o = acc_ref[r1, :] / l_ref[r1, :] - lam * (acc_ref[r2, :] / l_ref[r2, :])
        o_ref[:, h * 128:(h + 1) * 128] = _sub_norm(o, gain, lam_init)


def _decode_attn(layer, page_table_flat, qrows, cache_kt, cache_v, knew_t, vnew, lam_p, subln,
                 n_seq, n_new, n_pages):
    n_iters = n_pages // DECODE_PAGES_PER_ITER
    n_rows = 2 * ATTN_HEADS * n_new
    seq3 = lambda b, pt: (b, 0, 0)
    kern = functools.partial(_decode_attn_kernel, layer=layer, n_new=n_new, n_iters=n_iters, n_seq=n_seq)
    grid_spec = pltpu.PrefetchScalarGridSpec(
        num_scalar_prefetch=1,
        grid=(n_seq,),
        in_specs=[
            pl.BlockSpec((None, n_rows, GROUP_WIDTH), seq3),
            pl.BlockSpec((None, GROUP_WIDTH, PAGE_SIZE), seq3),
            pl.BlockSpec((None, GROUP_WIDTH, PAGE_SIZE), seq3),
            pl.BlockSpec((None, 8, 128), lambda b, pt: (layer, 0, 0)),
            pl.BlockSpec((None, 1, 128), lambda b, pt: (layer, 0, 0)),
            pl.BlockSpec(memory_space=pl.ANY),
            pl.BlockSpec(memory_space=pl.ANY),
        ],
        out_specs=pl.BlockSpec((n_new, GROUP_WIDTH), lambda b, pt: (b, 0)),
        scratch_shapes=[
            pltpu.VMEM((DECODE_SLOTS, GROUP_WIDTH, PAGE_SIZE), F32),
            pltpu.VMEM((DECODE_SLOTS, GROUP_WIDTH, PAGE_SIZE), F32),
            pltpu.SemaphoreType.DMA((2, DECODE_SLOTS)),
            pltpu.VMEM((n_rows, 128), F32),
            pltpu.VMEM((n_rows, 128), F32),
            pltpu.VMEM((n_rows, 128), F32),
        ],
    )
    return pl.pallas_call(
        kern,
        grid_spec=grid_spec,
        out_shape=jax.ShapeDtypeStruct((n_seq * n_new, GROUP_WIDTH), F32),
        compiler_params=_params("arbitrary"),
        name="decode_attn",
    )(page_table_flat, qrows, knew_t, vnew, lam_p, subln, cache_kt, cache_v)


def _out_ffn_kernel(h_ref, ys_ref, yp_ref, ya_ref, wo_ref, g_ref, wg_ref, wu_ref, wd_ref,
                    o_ref, acc_ref, u_ref):
    f = pl.program_id(1)

    @pl.when(f == 0)
    def _():
        h1 = h_ref[...]
        h1 = h1 + _dot(ys_ref[...].astype(BF16), wo_ref[0:GROUP_WIDTH, :])
        h1 = h1 + _dot(yp_ref[...].astype(BF16), wo_ref[GROUP_WIDTH:2 * GROUP_WIDTH, :])
        h1 = h1 + _dot(ya_ref[...].astype(BF16), wo_ref[2 * GROUP_WIDTH:3 * GROUP_WIDTH, :])
        acc_ref[...] = h1
        ms = jnp.mean(h1 * h1, axis=-1, keepdims=True)
        u_ref[...] = (h1 * lax.rsqrt(ms + RMS_EPS) * g_ref[...]).astype(BF16)

    u = u_ref[...]
    act = (_silu(_dot(u, wg_ref[...])) * _dot(u, wu_ref[...])).astype(BF16)
    acc_ref[...] += _dot(act, wd_ref[...])

    @pl.when(f == pl.num_programs(1) - 1)
    def _():
        o_ref[...] = acc_ref[...]


def _out_ffn(layer, h, ys, yp, ya, w_out_b, norm_ffn, w_gu_b, w_down_b, tm, tf):
    n_tok, d = h.shape
    ffn = w_down_b.shape[1]
    nf = ffn // tf
    row = lambda i, f: (i, 0)
    return pl.pallas_call(
        _out_ffn_kernel,
        grid=(n_tok // tm, nf),
        in_specs=[
            pl.BlockSpec((tm, d), row),
            pl.BlockSpec((tm, GROUP_WIDTH), row),
            pl.BlockSpec((tm, GROUP_WIDTH), row),
            pl.BlockSpec((tm, GROUP_WIDTH), row),
            pl.BlockSpec((None, 3 * GROUP_WIDTH, d), lambda i, f: (layer, 0, 0)),
            pl.BlockSpec((None, 1, d), lambda i, f: (layer, 0, 0)),
            pl.BlockSpec((None, d, tf), lambda i, f: (layer, 0, f)),
            pl.BlockSpec((None, d, tf), lambda i, f: (layer, 0, nf + f)),
            pl.BlockSpec((None, tf, d), lambda i, f: (layer, f, 0)),
        ],
        out_specs=pl.BlockSpec((tm, d), row),
        out_shape=jax.ShapeDtypeStruct((n_tok, d), F32),
        scratch_shapes=[pltpu.VMEM((tm, d), F32), pltpu.VMEM((tm, d), BF16)],
        compiler_params=_params("parallel", "arbitrary"),
        name="out_ffn",
    )(h, ys, yp, ya, w_out_b, norm_ffn, w_gu_b, w_gu_b, w_down_b)


def _rope_tables(pos):
    half = HEAD_DIM // 2
    inv_freq = ROPE_THETA ** (-jnp.arange(half, dtype=F32) / half)
    ang = pos.astype(F32)[:, None] * inv_freq[None, :]
    return jnp.tile(jnp.cos(ang), (1, 128 // half)), jnp.tile(jnp.sin(ang), (1, 128 // half))


def _pad_rows_front(x, rows):
    pad = jnp.zeros(x.shape[:-2] + (rows - x.shape[-2], x.shape[-1]), x.dtype)
    return jnp.concatenate([pad, x], axis=-2)


def kernel(x_prompt, x_sample, cache_k, cache_v, state_conv, state_ssm, state_pool, page_table, norm_mix, w_in, conv_w, conv_b, dt_bias, a_log, d_skip, ssd_norm, pool_w, pool_scale, q_norm, k_norm, lam_q1, lam_k1, lam_q2, lam_k2, subln, w_out, norm_ffn, w_gate_up, w_down):
    depth = w_in.shape[0]
    bp, lp, d = x_prompt.shape
    bs, ls, _ = x_sample.shape
    n_pages = page_table.shape[1]
    past_len = n_pages * PAGE_SIZE
    ffn = w_down.shape[1]

    c0 = GROUP_WIDTH
    c1 = c0 + GROUP_WIDTH
    c2 = c1 + 256
    c3 = c2 + SSD_HEADS
    w_in_t = jnp.transpose(w_in, (0, 2, 1))
    w_in_p = jnp.concatenate(
        [w_in_t[:, 0:c1], w_in_t[:, c3:], w_in_t[:, c1:c2], w_in_t[:, c2:c3],
         jnp.zeros((depth, 128 - SSD_HEADS, w_in.shape[1]), w_in.dtype)], axis=1).astype(BF16)
    w_out_b = w_out.astype(BF16)
    w_gu_b = w_gate_up.astype(BF16)
    w_down_b = w_down.astype(BF16)
    pool_w_b = pool_w.astype(BF16)
    norm_mix3 = norm_mix[:, None, :]
    norm_ffn3 = norm_ffn[:, None, :]
    q_gain = jnp.tile(q_norm, (1, 4))[:, None, :]
    k_gain = jnp.tile(k_norm, (1, 4))[:, None, :]
    blk = jnp.arange(256) // HEAD_DIM
    seg_mat = ((blk[:, None] == blk[None, :]).astype(F32) / HEAD_DIM).astype(BF16)
    cw = jnp.concatenate([conv_w, jnp.zeros((depth, 8 - CONV_TAPS, conv_w.shape[-1]), F32)], axis=1)
    cwx, cwbc = cw[..., :GROUP_WIDTH], cw[..., GROUP_WIDTH:]
    cbiasx = conv_b[:, None, :GROUP_WIDTH]
    cbiasbc = conv_b[:, None, GROUP_WIDTH:]
    pad_heads = lambda x: jnp.concatenate([x, jnp.zeros((depth, 128 - SSD_HEADS), x.dtype)], axis=-1)[:, None, :]
    dtb = pad_heads(dt_bias)
    alog = pad_heads(a_log)
    dskip = jnp.repeat(d_skip, HEAD_DIM, axis=-1)[:, None, :]
    ssd_gain = ssd_norm[:, None, :]
    head_expand = (jnp.arange(128)[:, None] == (jnp.arange(GROUP_WIDTH) // HEAD_DIM)[None, :]).astype(BF16)
    pool_scale3 = pool_scale[:, None, :]
    subln3 = subln[:, None, :]
    lam_inits = jnp.asarray([0.8 - 0.6 * math.exp(-0.3 * l) for l in range(depth)], F32)
    lam_rows = jnp.stack([lam_q1, lam_k1, lam_q2, lam_k2], axis=1)
    lam_rows = jnp.concatenate([lam_rows, jnp.zeros((depth, 4, 128 - HEAD_DIM), F32)], axis=-1)
    lam_p = jnp.concatenate(
        [lam_rows, jnp.broadcast_to(lam_inits[:, None, None], (depth, 1, 128)),
         jnp.zeros((depth, 3, 128), F32)], axis=1)

    score_bound = HEAD_DIM * QK_SCALE_LOG2 * jnp.max(jnp.abs(q_norm), axis=-1) * jnp.max(jnp.abs(k_norm), axis=-1)
    safe_flags = (score_bound < SAFE_SCORE_BOUND).astype(jnp.int32)

    cos_p, sin_p = _rope_tables(jnp.arange(lp, dtype=jnp.int32))
    pos_s = past_len + jnp.arange(ls, dtype=jnp.int32)
    cos_s, sin_s = _rope_tables(jnp.tile(pos_s, bs))

    zc = jnp.zeros((bp, 8, GROUP_WIDTH + 256), F32)
    cb_s = _pad_rows_front(state_conv, 8)
    h0_p = jnp.zeros((bp, HEAD_DIM, GROUP_WIDTH), F32)
    h0_s = jnp.transpose(state_ssm, (0, 1, 4, 2, 3)).reshape(depth, bs, HEAD_DIM, GROUP_WIDTH)
    pb_p = jnp.zeros((bp, 16, GROUP_WIDTH), F32)
    pb_s = _pad_rows_front(state_pool, 16)
    page_flat = page_table.reshape(-1).astype(jnp.int32)
    n_pool = cache_k.shape[1]
    cache_kt = jnp.transpose(cache_k, (0, 1, 3, 4, 5, 2)).reshape(depth, n_pool, GROUP_WIDTH, PAGE_SIZE)
    cache_v2 = cache_v.reshape(depth, n_pool, PAGE_SIZE * ATTN_HEADS, 2 * HEAD_DIM)
    eye_hc = jnp.eye(2 * ATTN_HEADS, dtype=BF16)

    hp = x_prompt.reshape(bp * lp, d)
    hs = x_sample.reshape(bs * ls, d)
    tm_p = 512 if (bp * lp) % 512 == 0 and lp % 512 == 0 else lp
    tm_s = bs * ls
    tf = ffn // 4 if (ffn // 4) % 128 == 0 else ffn

    outs_p = [[], [], [], [], []]
    outs_s = [[], [], [], [], []]

    def state_outputs(k, v, ncx, ncbc, hout, nbuf, n_seq, seq_len, store):
        store[0].append(k.reshape(n_seq, seq_len, ATTN_HEADS, 2, HEAD_DIM))
        store[1].append(v.reshape(n_seq, seq_len, ATTN_HEADS, 2 * HEAD_DIM))
        store[2].append(jnp.concatenate([ncx, ncbc], axis=-1)[:, 8 - (CONV_TAPS - 1):, :])
        store[3].append(jnp.transpose(hout.reshape(n_seq, HEAD_DIM, SSD_HEADS, HEAD_DIM), (0, 2, 3, 1)))
        store[4].append(nbuf[:, 16 - POOL_BUF:, :])

    for l in range(depth):
        ssd_w = (cwx, cwbc, cbiasx, cbiasbc, dtb, alog, dskip, ssd_gain, head_expand)
        z, xs, pin, qb, k, v, kb, vb, bc, dt = _in_proj(l, hp, norm_mix3, w_in_p, cos_p, sin_p, q_gain, k_gain,
                                                        seg_mat, tm_p, n_seq=bp)
        k = jnp.transpose(k.reshape(bp, ATTN_HEADS, 2, HEAD_DIM, lp), (0, 4, 1, 2, 3))
        y_ssd, ncx, ncbc, hout = _ssd(l, z, xs, bc, dt, zc[..., :GROUP_WIDTH], zc[..., GROUP_WIDTH:], h0_p,
                                      *ssd_w, n_seq=bp, seq_len=lp)
        y_pool, nbuf = _pool(l, pin, pb_p, pool_w_b, pool_scale3, bp, lp, 0)
        y_attn = _prompt_attn(l, safe_flags, qb, kb, vb, lam_p, subln3, bp, lp)
        hp = _out_ffn(l, hp, y_ssd, y_pool, y_attn, w_out_b, norm_ffn3, w_gu_b, w_down_b, tm_p, tf)
        state_outputs(k, v, ncx, ncbc, hout, nbuf, bp, lp, outs_p)

        z, xs, pin, qb, k, v, kb, vb, bc, dt = _in_proj(l, hs, norm_mix3, w_in_p, cos_s, sin_s, q_gain, k_gain,
                                                        seg_mat, tm_s)
        y_ssd, ncx, ncbc, hout = _ssd(l, z, xs, bc, dt, cb_s[l, ..., :GROUP_WIDTH], cb_s[l, ..., GROUP_WIDTH:],
                                      h0_s[l], *ssd_w, n_seq=bs, seq_len=ls)
        y_pool, nbuf = _pool(l, pin, pb_s[l], pool_w_b, pool_scale3, bs, ls, past_len)
        q4 = qb.reshape(bs, ls, 2 * ATTN_HEADS, HEAD_DIM)
        qrows = jnp.transpose(q4[:, :, :, None, :] * eye_hc[None, None, :, :, None], (0, 2, 1, 3, 4))
        qrows = qrows.reshape(bs, 2 * ATTN_HEADS * ls, GROUP_WIDTH)
        knew_t = jnp.transpose(k.reshape(bs, ls, GROUP_WIDTH), (0, 2, 1))
        knew_t = jnp.concatenate([knew_t, jnp.zeros((bs, GROUP_WIDTH, PAGE_SIZE - ls), F32)], axis=2)
        vnew = v.reshape(bs, ls * ATTN_HEADS, 2 * HEAD_DIM)
        vnew = jnp.concatenate([vnew, jnp.zeros((bs, (PAGE_SIZE - ls) * ATTN_HEADS, 2 * HEAD_DIM), F32)], axis=1)
        y_attn = _decode_attn(l, page_flat, qrows, cache_kt, cache_v2, knew_t, vnew, lam_p, subln3,
                              bs, ls, n_pages)
        hs = _out_ffn(l, hs, y_ssd, y_pool, y_attn, w_out_b, norm_ffn3, w_gu_b, w_down_b, tm_s, tf)
        state_outputs(k, v, ncx, ncbc, hout, nbuf, bs, ls, outs_s)

    stacked_p = [jnp.stack(s, axis=0) for s in outs_p]
    stacked_s = [jnp.stack(s, axis=0) for s in outs_s]
    return (hp.reshape(bp, lp, d), hs.reshape(bs, ls, d), *stacked_p, *stacked_s)
```

```python
import functools
import math

import jax
import jax.numpy as jnp
from jax import lax
from jax.experimental import pallas as pl
from jax.experimental.pallas import tpu as pltpu

F32 = jnp.float32
BF16 = jnp.bfloat16

RMS_EPS = 1e-6
ROPE_THETA = 10000.0
HEAD_DIM = 64
GROUP_WIDTH = 512
CONV_TAPS = 4
POOL_WINDOWS = (2, 4, 8, 16)
POOL_BUF = 15
SSD_CHUNK = 128
SSD_HEADS = 8
SSD_GROUPS = 2
ATTN_HEADS = 4
PAGE_SIZE = 128
NEG_BIG = -1e30
QK_SCALE_LOG2 = (HEAD_DIM ** -0.5) * math.log2(math.e)
SAFE_SCORE_BOUND = 64.0

VMEM_LIMIT_BYTES = 56 * 1024 * 1024

OFF_Z, OFF_XS, OFF_POOL, OFF_Q, OFF_K, OFF_V, OFF_BC, OFF_DT = 0, 512, 1024, 1536, 2048, 2560, 3072, 3328
IN_PROJ_PADDED = 3456


def _params(*semantics):
    return pltpu.CompilerParams(dimension_semantics=semantics, vmem_limit_bytes=VMEM_LIMIT_BYTES)


def _silu(x):
    return x / (1.0 + jnp.exp(-x))


def _dot(a, b):
    return jnp.dot(a, b, preferred_element_type=F32)


def _dot_nt(a, b):
    return lax.dot_general(a, b, (((1,), (1,)), ((), ())), preferred_element_type=F32)


def _dot_tn(a, b):
    return lax.dot_general(a, b, (((0,), (0,)), ((), ())), preferred_element_type=F32)


def _split3(x):
    hi = x.astype(BF16)
    r1 = x - hi.astype(F32)
    mid = r1.astype(BF16)
    lo = (r1 - mid.astype(F32)).astype(BF16)
    return hi, mid, lo


def _in_proj_kernel(x_ref, g_ref, w_ref, cos_ref, sin_ref, qg_ref, kg_ref, seg_ref,
                    z_ref, xs_ref, pool_ref, qb_ref, k_ref, v_ref, kb_ref, vb_ref, bc_ref, dt_ref, *, k_transposed):
    x = x_ref[...]
    ms = jnp.mean(x * x, axis=-1, keepdims=True)
    u = (x * lax.rsqrt(ms + RMS_EPS) * g_ref[...]).astype(BF16)

    def proj(off, width):
        return _dot_nt(u, w_ref[off:off + width, :])

    z_ref[...] = proj(OFF_Z, GROUP_WIDTH)
    xs_ref[...] = proj(OFF_XS, GROUP_WIDTH)
    pool_ref[...] = proj(OFF_POOL, GROUP_WIDTH)
    v = proj(OFF_V, GROUP_WIDTH)
    for h in range(ATTN_HEADS):
        v_ref[pl.ds(h, v.shape[0], stride=ATTN_HEADS), :] = v[:, h * 128:(h + 1) * 128]
    vb_ref[...] = v.astype(BF16)
    bc_ref[...] = proj(OFF_BC, 256)
    dt_ref[...] = proj(OFF_DT, 128)

    cos = cos_ref[...]
    sin = sin_ref[...]
    seg = seg_ref[...]
    lane = lax.broadcasted_iota(jnp.int32, (1, 128), 1)
    first_half = (lane % HEAD_DIM) < (HEAD_DIM // 2)

    def norm_rope(off, gain_ref, out_ref, out_b_ref, scale):
        gain = gain_ref[...]
        for s in range(GROUP_WIDTH // 256):
            y = proj(off + s * 256, 256)
            seg_ms = _dot((y * y).astype(BF16), seg)
            yn_wide = y * lax.rsqrt(seg_ms + RMS_EPS) * gain
            for t in range(2):
                lo = s * 256 + t * 128
                yn = yn_wide[:, t * 128:(t + 1) * 128]
                rot = jnp.where(first_half, -pltpu.roll(yn, 128 - HEAD_DIM // 2, 1),
                                pltpu.roll(yn, HEAD_DIM // 2, 1))
                roped = yn * cos + rot * sin
                if out_ref is not None and k_transposed:
                    out_ref[lo:lo + 128, :] = roped.T
                elif out_ref is not None:
                    out_ref[:, lo:lo + 128] = roped
                out_b_ref[:, lo:lo + 128] = (roped * scale).astype(BF16)

    norm_rope(OFF_Q, qg_ref, None, qb_ref, QK_SCALE_LOG2)
    norm_rope(OFF_K, kg_ref, k_ref, kb_ref, 1.0)


V_OUT = 5


K_OUT = 4


def _in_proj(layer, x, norm_mix, w_in_p, cos_t, sin_t, q_gain, k_gain, seg_mat, tm, n_seq=None):
    n_tok = x.shape[0]
    n_pos_blocks = cos_t.shape[0] // tm
    row = lambda i: (i, 0)
    k_transposed = n_seq is not None

    def out_spec(o, w):
        if o == V_OUT:
            return pl.BlockSpec((tm * ATTN_HEADS, 128), row)
        if o == K_OUT and k_transposed:
            return pl.BlockSpec((None, w, tm), lambda i: (i // n_pos_blocks, 0, i % n_pos_blocks))
        return pl.BlockSpec((tm, w), row)

    def out_struct(o, w, dt):
        if o == V_OUT:
            return jax.ShapeDtypeStruct((n_tok * ATTN_HEADS, 128), dt)
        if o == K_OUT and k_transposed:
            return jax.ShapeDtypeStruct((n_seq, w, cos_t.shape[0]), dt)
        return jax.ShapeDtypeStruct((n_tok, w), dt)

    widths = (GROUP_WIDTH,) * 8 + (256, 128)
    dtypes = (F32, F32, F32, BF16, F32, F32, BF16, BF16, F32, F32)
    return pl.pallas_call(
        functools.partial(_in_proj_kernel, k_transposed=k_transposed),
        grid=(n_tok // tm,),
        in_specs=[
            pl.BlockSpec((tm, x.shape[1]), row),
            pl.BlockSpec((None, 1, x.shape[1]), lambda i: (layer, 0, 0)),
            pl.BlockSpec((None, IN_PROJ_PADDED, x.shape[1]), lambda i: (layer, 0, 0)),
            pl.BlockSpec((tm, 128), lambda i: (i % n_pos_blocks, 0)),
            pl.BlockSpec((tm, 128), lambda i: (i % n_pos_blocks, 0)),
            pl.BlockSpec((None, 1, 256), lambda i: (layer, 0, 0)),
            pl.BlockSpec((None, 1, 256), lambda i: (layer, 0, 0)),
            pl.BlockSpec((256, 256), lambda i: (0, 0)),
        ],
        out_specs=[out_spec(o, w) for o, w in enumerate(widths)],
        out_shape=[out_struct(o, w, dt) for o, (w, dt) in enumerate(zip(widths, dtypes))],
        compiler_params=_params("parallel"),
        name="in_proj",
    )(x, norm_mix, w_in_p, cos_t, sin_t, q_gain, k_gain, seg_mat)


def _ssd_kernel(z_ref, xs_ref, bc_ref, dt_ref, cbx_ref, cbbc_ref, h0_ref,
                cwx_ref, cwbc_ref, cbiasx_ref, cbiasbc_ref, dtb_ref, alog_ref, dskip_ref, gain_ref, expand_ref,
                y_ref, ncx_ref, ncbc_ref, hout_ref,
                extx_ref, extbc_ref, h_ref, *, valid):
    Q = SSD_CHUNK
    c = pl.program_id(1)

    @pl.when(c == 0)
    def _():
        extx_ref[0:8, :] = cbx_ref[...]
        extbc_ref[0:8, :] = cbbc_ref[...]
        h_ref[...] = h0_ref[...]
        if valid < Q:
            extx_ref[8 + valid:8 + Q, :] = jnp.zeros((Q - valid, GROUP_WIDTH), F32)
            extbc_ref[8 + valid:8 + Q, :] = jnp.zeros((Q - valid, 256), F32)

    extx_ref[8:8 + valid, :] = xs_ref[...]
    extbc_ref[8:8 + valid, :] = bc_ref[...]

    def conv(ext_ref, w_ref, b_ref):
        acc = b_ref[...] + ext_ref[8:8 + Q, :] * w_ref[CONV_TAPS - 1:CONV_TAPS, :]
        for i in range(CONV_TAPS - 1):
            lo = 8 - (CONV_TAPS - 1) + i
            acc = acc + ext_ref[lo:lo + Q, :] * w_ref[i:i + 1, :]
        return _silu(acc)

    xs = conv(extx_ref, cwx_ref, cbiasx_ref)
    bc = conv(extbc_ref, cwbc_ref, cbiasbc_ref)
    tailx = extx_ref[valid:valid + 8, :]
    tailbc = extbc_ref[valid:valid + 8, :]
    extx_ref[0:8, :] = tailx
    extbc_ref[0:8, :] = tailbc
    ncx_ref[...] = tailx
    ncbc_ref[...] = tailbc

    dtr = dt_ref[...] + dtb_ref[...]
    dt = jnp.maximum(dtr, 0.0) + jnp.log1p(jnp.exp(-jnp.abs(dtr)))
    if valid < Q:
        dt = jnp.concatenate([dt, jnp.zeros((Q - valid, 128), F32)], axis=0)
    a = -jnp.exp(alog_ref[...])
    d_a = dt * a
    ri = lax.broadcasted_iota(jnp.int32, (Q, Q), 0)
    ci = lax.broadcasted_iota(jnp.int32, (Q, Q), 1)
    causal = ri >= ci
    tri = causal.astype(BF16)
    hi, mid, lo = _split3(d_a)
    cum = _dot(tri, hi) + _dot(tri, mid) + _dot(tri, lo)
    cum_t = cum.T
    dt_t = dt.T
    cum_last = cum[Q - 1:Q, :]
    e_cum = jnp.exp(cum)
    to_end = jnp.exp(cum_last - cum) * dt
    e_last = jnp.exp(cum_last)

    xs_b = xs.astype(BF16)
    h_prev = h_ref[...]
    h_prev_b = h_prev.astype(BF16)
    heads_per_group = SSD_HEADS // SSD_GROUPS
    gw = heads_per_group * HEAD_DIM
    expand = expand_ref[...]
    e_cum_x = _dot(e_cum.astype(BF16), expand)
    to_end_x = _dot(to_end.astype(BF16), expand)
    xw = (xs * to_end_x).astype(BF16)
    low_half = lax.broadcasted_iota(jnp.int32, (1, 128), 1) < HEAD_DIM
    y_off_parts = []
    y_diag_parts = []
    for g in range(SSD_GROUPS):
        b_g = bc[:, g * HEAD_DIM:(g + 1) * HEAD_DIM].astype(BF16)
        c_g = bc[:, 128 + g * HEAD_DIM:128 + (g + 1) * HEAD_DIM].astype(BF16)
        cb = _dot_nt(c_g, b_g)
        y_off_parts.append(_dot(c_g, h_prev_b[:, g * gw:(g + 1) * gw]))
        score_parts = []
        dec_parts = []
        for hh in range(heads_per_group):
            h = g * heads_per_group + hh
            seg = cum[:, h:h + 1] - cum_t[h:h + 1, :]
            decay = jnp.exp(jnp.where(causal, seg, NEG_BIG))
            score_parts.append((cb * decay * dt_t[h:h + 1, :]).astype(BF16))
            dec_parts.append(jnp.broadcast_to(e_last[:, h:h + 1], (1, HEAD_DIM)))
        for pr in range(heads_per_group // 2):
            lo = (g * heads_per_group + 2 * pr) * HEAD_DIM
            x_pair = xs_b[:, lo:lo + 128]
            zero = jnp.zeros_like(x_pair)
            x_bd = jnp.concatenate([jnp.where(low_half, x_pair, zero), jnp.where(low_half, zero, x_pair)], axis=0)
            s_pair = jnp.concatenate(score_parts[2 * pr:2 * pr + 2], axis=1)
            y_diag_parts.append(_dot(s_pair, x_bd))
        st_g = _dot_tn(b_g, xw[:, g * gw:(g + 1) * gw])
        dec_g = jnp.concatenate(dec_parts, axis=1)
        h_ref[:, g * gw:(g + 1) * gw] = dec_g * h_prev[:, g * gw:(g + 1) * gw] + st_g
    hout_ref[...] = h_ref[...]

    y = (jnp.concatenate(y_diag_parts, axis=1) + jnp.concatenate(y_off_parts, axis=1) * e_cum_x
         + dskip_ref[...] * xs)[0:valid, :]
    y = y * _silu(z_ref[...])
    ms = jnp.mean(y * y, axis=-1, keepdims=True)
    y_ref[...] = y * lax.rsqrt(ms + RMS_EPS) * gain_ref[...]


def _ssd(layer, z, xs, bc, dt, cbx, cbbc, h0t, cwx, cwbc, cbiasx, cbiasbc, dtb, alog, dskip, gain, expand,
         n_seq, seq_len):
    valid = min(seq_len, SSD_CHUNK)
    n_chunks = seq_len // valid
    tok = lambda b, c: (b * n_chunks + c, 0)
    seq3 = lambda b, c: (b, 0, 0)
    lay3 = lambda b, c: (layer, 0, 0)
    n_tok = n_seq * seq_len
    kern = functools.partial(_ssd_kernel, valid=valid)
    return pl.pallas_call(
        kern,
        grid=(n_seq, n_chunks),
        in_specs=[
            pl.BlockSpec((valid, GROUP_WIDTH), tok),
            pl.BlockSpec((valid, GROUP_WIDTH), tok),
            pl.BlockSpec((valid, 256), tok),
            pl.BlockSpec((valid, 128), tok),
            pl.BlockSpec((None, 8, GROUP_WIDTH), seq3),
            pl.BlockSpec((None, 8, 256), seq3),
            pl.BlockSpec((None, HEAD_DIM, GROUP_WIDTH), seq3),
            pl.BlockSpec((None, 8, GROUP_WIDTH), lay3),
            pl.BlockSpec((None, 8, 256), lay3),
            pl.BlockSpec((None, 1, GROUP_WIDTH), lay3),
            pl.BlockSpec((None, 1, 256), lay3),
            pl.BlockSpec((None, 1, 128), lay3),
            pl.BlockSpec((None, 1, 128), lay3),
            pl.BlockSpec((None, 1, GROUP_WIDTH), lay3),
            pl.BlockSpec((None, 1, GROUP_WIDTH), lay3),
            pl.BlockSpec((128, GROUP_WIDTH), lambda b, c: (0, 0)),
        ],
        out_specs=[
            pl.BlockSpec((valid, GROUP_WIDTH), tok),
            pl.BlockSpec((None, 8, GROUP_WIDTH), seq3),
            pl.BlockSpec((None, 8, 256), seq3),
            pl.BlockSpec((None, HEAD_DIM, GROUP_WIDTH), seq3),
        ],
        out_shape=[
            jax.ShapeDtypeStruct((n_tok, GROUP_WIDTH), F32),
            jax.ShapeDtypeStruct((n_seq, 8, GROUP_WIDTH), F32),
            jax.ShapeDtypeStruct((n_seq, 8, 256), F32),
            jax.ShapeDtypeStruct((n_seq, HEAD_DIM, GROUP_WIDTH), F32),
        ],
        scratch_shapes=[
            pltpu.VMEM((8 + SSD_CHUNK, GROUP_WIDTH), F32),
            pltpu.VMEM((8 + SSD_CHUNK, 256), F32),
            pltpu.VMEM((HEAD_DIM, GROUP_WIDTH), F32),
        ],
        compiler_params=_params("parallel", "arbitrary"),
        name="ssd",
    )(z, xs, bc, dt, cbx, cbbc, h0t, cwx, cwbc, cbiasx, cbiasbc, dtb, alog, dskip, gain, expand)


def _pool_kernel(x_ref, buf_ref, w_ref, scale_ref, y_ref, nbuf_ref, ext_ref, *, tl, pos0):
    c = pl.program_id(1)

    @pl.when(c == 0)
    def _():
        ext_ref[0:16, :] = buf_ref[...]

    ext_ref[16:16 + tl, :] = x_ref[...]
    row = lax.broadcasted_iota(jnp.int32, (tl, 1), 0)
    pos = pos0 + c * tl + row
    for g, w in enumerate(POOL_WINDOWS):
        sl = slice(g * 128, (g + 1) * 128)
        cur = ext_ref[16:16 + tl, sl]
        win = cur
        for k in range(1, w):
            win = win + ext_ref[16 - k:16 - k + tl, sl]
        cnt = jnp.minimum(w, pos + 1).astype(F32)
        pooled = (win / cnt - cur).astype(BF16)
        y_ref[:, sl] = _dot(pooled, w_ref[g]) * scale_ref[:, sl]
    tail = ext_ref[tl:tl + 16, :]
    ext_ref[0:16, :] = tail
    nbuf_ref[...] = tail


def _pool(layer, x, buf, pool_w, pool_scale, n_seq, seq_len, pos0):
    tl = min(seq_len, 512)
    n_chunks = seq_len // tl
    tok = lambda b, c: (b * n_chunks + c, 0)
    kern = functools.partial(_pool_kernel, tl=tl, pos0=pos0)
    return pl.pallas_call(
        kern,
        grid=(n_seq, n_chunks),
        in_specs=[
            pl.BlockSpec((tl, GROUP_WIDTH), tok),
            pl.BlockSpec((None, 16, GROUP_WIDTH), lambda b, c: (b, 0, 0)),
            pl.BlockSpec((None, 4, 128, 128), lambda b, c: (layer, 0, 0, 0)),
            pl.BlockSpec((None, 1, GROUP_WIDTH), lambda b, c: (layer, 0, 0)),
        ],
        out_specs=[
            pl.BlockSpec((tl, GROUP_WIDTH), tok),
            pl.BlockSpec((None, 16, GROUP_WIDTH), lambda b, c: (b, 0, 0)),
        ],
        out_shape=[
            jax.ShapeDtypeStruct((n_seq * seq_len, GROUP_WIDTH), F32),
            jax.ShapeDtypeStruct((n_seq, 16, GROUP_WIDTH), F32),
        ],
        scratch_shapes=[pltpu.VMEM((16 + tl, GROUP_WIDTH), F32)],
        compiler_params=_params("parallel", "arbitrary"),
        name="pool",
    )(x, buf, pool_w, pool_scale)


def _lam_value(lam_ref):
    lp = lam_ref[...]
    s1 = jnp.sum(lp[0:1, :] * lp[1:2, :], axis=-1, keepdims=True)
    s2 = jnp.sum(lp[2:3, :] * lp[3:4, :], axis=-1, keepdims=True)
    lam_init = lp[4:5, 0:1]
    return jnp.exp(s1) - jnp.exp(s2) + lam_init, lam_init


def _sub_norm(o, gain, lam_init):
    ms = jnp.mean(o * o, axis=-1, keepdims=True)
    return o * lax.rsqrt(ms + RMS_EPS) * gain * (1.0 - lam_init)


ATTN_BLOCK = 256


def _prompt_attn_kernel(flag_ref, q_ref, k_ref, v_ref, lam_ref, gain_ref, o_ref, m_ref, acc_ref, *, layer):
    T = ATTN_BLOCK
    i = pl.program_id(1)
    ones = jnp.ones((T, 128), BF16)

    def scores(j, hc, masked):
        start = pl.multiple_of(j * T, T)
        sl = slice(hc * HEAD_DIM, (hc + 1) * HEAD_DIM)
        s = _dot_nt(q_ref[:, sl], k_ref[pl.ds(start, T), sl])
        if masked:
            ri = lax.broadcasted_iota(jnp.int32, (T, T), 0)
            ci = lax.broadcasted_iota(jnp.int32, (T, T), 1)
            s = jnp.where(ci <= ri, s, NEG_BIG)
        return s

    m_ref[...] = jnp.zeros(m_ref.shape, F32)

    @pl.when(flag_ref[layer] == 0)
    def _():
        def fold_max(j, masked):
            for hc in range(2 * ATTN_HEADS):
                s = scores(j, hc, masked)
                m_ref[hc] = jnp.maximum(m_ref[hc], jnp.maximum(s[:, :128], s[:, 128:]))

        m_ref[...] = jnp.full(m_ref.shape, NEG_BIG, F32)
        fold_max(i, True)

        def body(j, carry):
            fold_max(j, False)
            return carry
        lax.fori_loop(0, i, body, 0)
        for hc in range(2 * ATTN_HEADS):
            m_ref[hc] = jnp.broadcast_to(jnp.max(m_ref[hc], axis=-1, keepdims=True), (T, 128))

    def accumulate(j, masked, first):
        start = pl.multiple_of(j * T, T)
        for h in range(ATTN_HEADS):
            v_aug = jnp.concatenate([v_ref[pl.ds(start, T), h * 128:(h + 1) * 128], ones], axis=1)
            for c in range(2):
                hc = 2 * h + c
                shift = jnp.tile(m_ref[hc], (1, T // 128))
                p = jnp.exp2(scores(j, hc, masked) - shift).astype(BF16)
                pv = _dot(p, v_aug)
                if first:
                    acc_ref[hc] = pv
                else:
                    acc_ref[hc] += pv

    accumulate(i, True, True)

    def body(t, carry):
        accumulate(2 * t, False, False)
        accumulate(2 * t + 1, False, False)
        return carry
    lax.fori_loop(0, i // 2, body, 0)

    @pl.when(i % 2 == 1)
    def _():
        accumulate(i - 1, False, False)

    lam, lam_init = _lam_value(lam_ref)
    gain = gain_ref[...]
    for h in range(ATTN_HEADS):
        a1 = acc_ref[2 * h]
        a2 = acc_ref[2 * h + 1]
        o = a1[:, :128] / a1[:, 128:] - lam * (a2[:, :128] / a2[:, 128:])
        o_ref[:, h * 128:(h + 1) * 128] = _sub_norm(o, gain, lam_init)


def _prompt_attn(layer, safe_flags, q, k, v, lam_p, subln, n_seq, seq_len):
    T = ATTN_BLOCK
    nq = seq_len // T
    kern = functools.partial(_prompt_attn_kernel, layer=layer)
    grid_spec = pltpu.PrefetchScalarGridSpec(
        num_scalar_prefetch=1,
        grid=(n_seq, nq),
        in_specs=[
            pl.BlockSpec((T, GROUP_WIDTH), lambda b, i, f: (b * nq + i, 0)),
            pl.BlockSpec((seq_len, GROUP_WIDTH), lambda b, i, f: (b, 0)),
            pl.BlockSpec((seq_len, GROUP_WIDTH), lambda b, i, f: (b, 0)),
            pl.BlockSpec((None, 8, 128), lambda b, i, f: (layer, 0, 0)),
            pl.BlockSpec((None, 1, 128), lambda b, i, f: (layer, 0, 0)),
        ],
        out_specs=pl.BlockSpec((T, GROUP_WIDTH), lambda b, i, f: (b * nq + i, 0)),
        scratch_shapes=[
            pltpu.VMEM((2 * ATTN_HEADS, T, 128), F32),
            pltpu.VMEM((2 * ATTN_HEADS, T, 256), F32),
        ],
    )
    return pl.pallas_call(
        kern,
        grid_spec=grid_spec,
        out_shape=jax.ShapeDtypeStruct((n_seq * seq_len, GROUP_WIDTH), F32),
        compiler_params=_params("parallel", "arbitrary"),
        name="prompt_attn",
    )(safe_flags, q, k, v, lam_p, subln)


DECODE_PAGES_PER_ITER = 8
DECODE_SLOTS = 2 * DECODE_PAGES_PER_ITER


def _decode_attn_kernel(pt_ref, qrows_ref, knew_ref, vnew_ref, lam_ref, gain_ref, kt_hbm, v_hbm, o_ref,
                        kbuf, vbuf, sem, m_ref, l_ref, acc_ref, *, layer, n_new, n_iters, n_seq):
    P = DECODE_PAGES_PER_ITER
    b = pl.program_id(0)
    total_iters = n_seq * n_iters
    n_rows = 2 * ATTN_HEADS * n_new

    def page_copies(g):
        base = (g % 2) * P
        copies = []
        for s in range(P):
            page = pt_ref[g * P + s]
            copies.append(pltpu.make_async_copy(kt_hbm.at[layer, page], kbuf.at[base + s], sem.at[0, base + s]))
            copies.append(pltpu.make_async_copy(v_hbm.at[layer, page], vbuf.at[base + s], sem.at[1, base + s]))
        return copies

    def start_iteration(g):
        for c in page_copies(g):
            c.start()

    @pl.when(b == 0)
    def _():
        start_iteration(0)
        start_iteration(1)

    m_ref[...] = jnp.full(m_ref.shape, NEG_BIG, F32)
    l_ref[...] = jnp.zeros(l_ref.shape, F32)
    acc_ref[...] = jnp.zeros(acc_ref.shape, F32)
    qrows = qrows_ref[...]

    def head_values(v_ref, h):
        return v_ref[pl.ds(h, PAGE_SIZE, stride=ATTN_HEADS), :].astype(BF16)

    def update(s_list, v_ref_list):
        m_prev = m_ref[...]
        m_new = m_prev
        for s in s_list:
            m_new = jnp.maximum(m_new, jnp.max(s, axis=-1, keepdims=True))
        alpha = jnp.exp2(m_prev - m_new)
        l_new = alpha * l_ref[...]
        pv = [jnp.zeros((2 * n_new, 128), F32) for _ in range(ATTN_HEADS)]
        for s, v_ref in zip(s_list, v_ref_list):
            p = jnp.exp2(s - m_new)
            l_new = l_new + jnp.sum(p, axis=-1, keepdims=True)
            p_b = p.astype(BF16)
            for h in range(ATTN_HEADS):
                rows = slice(h * 2 * n_new, (h + 1) * 2 * n_new)
                pv[h] = pv[h] + _dot(p_b[rows, :], head_values(v_ref, h))
        for h in range(ATTN_HEADS):
            rows = slice(h * 2 * n_new, (h + 1) * 2 * n_new)
            acc_ref[rows, :] = alpha[rows, :] * acc_ref[rows, :] + pv[h]
        l_ref[...] = l_new
        m_ref[...] = m_new

    def iteration(t, carry):
        g = b * n_iters + t
        base = (g % 2) * P
        for c in page_copies(g):
            c.wait()
        s_list = [_dot(qrows, kbuf[base + s].astype(BF16)) for s in range(P)]
        update(s_list, [vbuf.at[base + s] for s in range(P)])

        @pl.when(g + 2 < total_iters)
        def _():
            start_iteration(g + 2)
        return carry

    lax.fori_loop(0, n_iters, iteration, 0)

    s = _dot(qrows, knew_ref[...].astype(BF16))
    key = lax.broadcasted_iota(jnp.int32, (n_rows, PAGE_SIZE), 1)
    tok = lax.broadcasted_iota(jnp.int32, (n_rows, PAGE_SIZE), 0) % n_new
    s = jnp.where(key <= tok, s, NEG_BIG)
    update([s], [vnew_ref])
    lam, lam_init = _lam_value(lam_ref)
    gain = gain_ref[...]
    for h in range(ATTN_HEADS):
        r1 = slice(h * 2 * n_new, h * 2 * n_new + n_new)
        r2 = slice(h * 2 * n_new + n_new, (h + 1) * 2 * n_new)
        o = acc_ref[r1, :] / l_ref[r1, :] - lam * (acc_ref[r2, :] / l_ref[r2, :])
        o_ref[:, h * 128:(h + 1) * 128] = _sub_norm(o, gain, lam_init)


def _decode_attn(layer, page_table_flat, qrows, cache_kt, cache_v, knew_t, vnew, lam_p, subln,
                 n_seq, n_new, n_pages):
    n_iters = n_pages // DECODE_PAGES_PER_ITER
    n_rows = 2 * ATTN_HEADS * n_new
    seq3 = lambda b, pt: (b, 0, 0)
    kern = functools.partial(_decode_attn_kernel, layer=layer, n_new=n_new, n_iters=n_iters, n_seq=n_seq)
    grid_spec = pltpu.PrefetchScalarGridSpec(
        num_scalar_prefetch=1,
        grid=(n_seq,),
        in_specs=[
            pl.BlockSpec((None, n_rows, GROUP_WIDTH), seq3),
            pl.BlockSpec((None, GROUP_WIDTH, PAGE_SIZE), seq3),
            pl.BlockSpec((None, GROUP_WIDTH, PAGE_SIZE), seq3),
            pl.BlockSpec((None, 8, 128), lambda b, pt: (layer, 0, 0)),
            pl.BlockSpec((None, 1, 128), lambda b, pt: (layer, 0, 0)),
            pl.BlockSpec(memory_space=pl.ANY),
            pl.BlockSpec(memory_space=pl.ANY),
        ],
        out_specs=pl.BlockSpec((n_new, GROUP_WIDTH), lambda b, pt: (b, 0)),
        scratch_shapes=[
            pltpu.VMEM((DECODE_SLOTS, GROUP_WIDTH, PAGE_SIZE), F32),
            pltpu.VMEM((DECODE_SLOTS, GROUP_WIDTH, PAGE_SIZE), F32),
            pltpu.SemaphoreType.DMA((2, DECODE_SLOTS)),
            pltpu.VMEM((n_rows, 128), F32),
            pltpu.VMEM((n_rows, 128), F32),
            pltpu.VMEM((n_rows, 128), F32),
        ],
    )
    return pl.pallas_call(
        kern,
        grid_spec=grid_spec,
        out_shape=jax.ShapeDtypeStruct((n_seq * n_new, GROUP_WIDTH), F32),
        compiler_params=_params("arbitrary"),
        name="decode_attn",
    )(page_table_flat, qrows, knew_t, vnew, lam_p, subln, cache_kt, cache_v)


def _out_ffn_kernel(h_ref, ys_ref, yp_ref, ya_ref, wo_ref, g_ref, wg_ref, wu_ref, wd_ref,
                    o_ref, acc_ref, u_ref):
    f = pl.program_id(1)

    @pl.when(f == 0)
    def _():
        h1 = h_ref[...]
        h1 = h1 + _dot(ys_ref[...].astype(BF16), wo_ref[0:GROUP_WIDTH, :])
        h1 = h1 + _dot(yp_ref[...].astype(BF16), wo_ref[GROUP_WIDTH:2 * GROUP_WIDTH, :])
        h1 = h1 + _dot(ya_ref[...].astype(BF16), wo_ref[2 * GROUP_WIDTH:3 * GROUP_WIDTH, :])
        acc_ref[...] = h1
        ms = jnp.mean(h1 * h1, axis=-1, keepdims=True)
        u_ref[...] = (h1 * lax.rsqrt(ms + RMS_EPS) * g_ref[...]).astype(BF16)

    u = u_ref[...]
    act = (_silu(_dot(u, wg_ref[...])) * _dot(u, wu_ref[...])).astype(BF16)
    acc_ref[...] += _dot(act, wd_ref[...])

    @pl.when(f == pl.num_programs(1) - 1)
    def _():
        o_ref[...] = acc_ref[...]


def _out_ffn(layer, h, ys, yp, ya, w_out_b, norm_ffn, w_gu_b, w_down_b, tm, tf):
    n_tok, d = h.shape
    ffn = w_down_b.shape[1]
    nf = ffn // tf
    row = lambda i, f: (i, 0)
    return pl.pallas_call(
        _out_ffn_kernel,
        grid=(n_tok // tm, nf),
        in_specs=[
            pl.BlockSpec((tm, d), row),
            pl.BlockSpec((tm, GROUP_WIDTH), row),
            pl.BlockSpec((tm, GROUP_WIDTH), row),
            pl.BlockSpec((tm, GROUP_WIDTH), row),
            pl.BlockSpec((None, 3 * GROUP_WIDTH, d), lambda i, f: (layer, 0, 0)),
            pl.BlockSpec((None, 1, d), lambda i, f: (layer, 0, 0)),
            pl.BlockSpec((None, d, tf), lambda i, f: (layer, 0, f)),
            pl.BlockSpec((None, d, tf), lambda i, f: (layer, 0, nf + f)),
            pl.BlockSpec((None, tf, d), lambda i, f: (layer, f, 0)),
        ],
        out_specs=pl.BlockSpec((tm, d), row),
        out_shape=jax.ShapeDtypeStruct((n_tok, d), F32),
        scratch_shapes=[pltpu.VMEM((tm, d), F32), pltpu.VMEM((tm, d), BF16)],
        compiler_params=_params("parallel", "arbitrary"),
        name="out_ffn",
    )(h, ys, yp, ya, w_out_b, norm_ffn, w_gu_b, w_gu_b, w_down_b)


def _rope_tables(pos):
    half = HEAD_DIM // 2
    inv_freq = ROPE_THETA ** (-jnp.arange(half, dtype=F32) / half)
    ang = pos.astype(F32)[:, None] * inv_freq[None, :]
    return jnp.tile(jnp.cos(ang), (1, 128 // half)), jnp.tile(jnp.sin(ang), (1, 128 // half))


def _pad_rows_front(x, rows):
    pad = jnp.zeros(x.shape[:-2] + (rows - x.shape[-2], x.shape[-1]), x.dtype)
    return jnp.concatenate([pad, x], axis=-2)


def kernel(x_prompt, x_sample, cache_k, cache_v, state_conv, state_ssm, state_pool, page_table, norm_mix, w_in, conv_w, conv_b, dt_bias, a_log, d_skip, ssd_norm, pool_w, pool_scale, q_norm, k_norm, lam_q1, lam_k1, lam_q2, lam_k2, subln, w_out, norm_ffn, w_gate_up, w_down):
    depth = w_in.shape[0]
    bp, lp, d = x_prompt.shape
    bs, ls, _ = x_sample.shape
    n_pages = page_table.shape[1]
    past_len = n_pages * PAGE_SIZE
    ffn = w_down.shape[1]

    c0 = GROUP_WIDTH
    c1 = c0 + GROUP_WIDTH
    c2 = c1 + 256
    c3 = c2 + SSD_HEADS
    w_in_t = jnp.transpose(w_in, (0, 2, 1))
    w_in_p = jnp.concatenate(
        [w_in_t[:, 0:c1], w_in_t[:, c3:], w_in_t[:, c1:c2], w_in_t[:, c2:c3],
         jnp.zeros((depth, 128 - SSD_HEADS, w_in.shape[1]), w_in.dtype)], axis=1).astype(BF16)
    w_out_b = w_out.astype(BF16)
    w_gu_b = w_gate_up.astype(BF16)
    w_down_b = w_down.astype(BF16)
    pool_w_b = pool_w.astype(BF16)
    norm_mix3 = norm_mix[:, None, :]
    norm_ffn3 = norm_ffn[:, None, :]
    q_gain = jnp.tile(q_norm, (1, 4))[:, None, :]
    k_gain = jnp.tile(k_norm, (1, 4))[:, None, :]
    blk = jnp.arange(256) // HEAD_DIM
    seg_mat = ((blk[:, None] == blk[None, :]).astype(F32) / HEAD_DIM).astype(BF16)
    cw = jnp.concatenate([conv_w, jnp.zeros((depth, 8 - CONV_TAPS, conv_w.shape[-1]), F32)], axis=1)
    cwx, cwbc = cw[..., :GROUP_WIDTH], cw[..., GROUP_WIDTH:]
    cbiasx = conv_b[:, None, :GROUP_WIDTH]
    cbiasbc = conv_b[:, None, GROUP_WIDTH:]
    pad_heads = lambda x: jnp.concatenate([x, jnp.zeros((depth, 128 - SSD_HEADS), x.dtype)], axis=-1)[:, None, :]
    dtb = pad_heads(dt_bias)
    alog = pad_heads(a_log)
    dskip = jnp.repeat(d_skip, HEAD_DIM, axis=-1)[:, None, :]
    ssd_gain = ssd_norm[:, None, :]
    head_expand = (jnp.arange(128)[:, None] == (jnp.arange(GROUP_WIDTH) // HEAD_DIM)[None, :]).astype(BF16)
    pool_scale3 = pool_scale[:, None, :]
    subln3 = subln[:, None, :]
    lam_inits = jnp.asarray([0.8 - 0.6 * math.exp(-0.3 * l) for l in range(depth)], F32)
    lam_rows = jnp.stack([lam_q1, lam_k1, lam_q2, lam_k2], axis=1)
    lam_rows = jnp.concatenate([lam_rows, jnp.zeros((depth, 4, 128 - HEAD_DIM), F32)], axis=-1)
    lam_p = jnp.concatenate(
        [lam_rows, jnp.broadcast_to(lam_inits[:, None, None], (depth, 1, 128)),
         jnp.zeros((depth, 3, 128), F32)], axis=1)

    score_bound = HEAD_DIM * QK_SCALE_LOG2 * jnp.max(jnp.abs(q_norm), axis=-1) * jnp.max(jnp.abs(k_norm), axis=-1)
    safe_flags = (score_bound < SAFE_SCORE_BOUND).astype(jnp.int32)

    cos_p, sin_p = _rope_tables(jnp.arange(lp, dtype=jnp.int32))
    pos_s = past_len + jnp.arange(ls, dtype=jnp.int32)
    cos_s, sin_s = _rope_tables(jnp.tile(pos_s, bs))

    zc = jnp.zeros((bp, 8, GROUP_WIDTH + 256), F32)
    cb_s = _pad_rows_front(state_conv, 8)
    h0_p = jnp.zeros((bp, HEAD_DIM, GROUP_WIDTH), F32)
    h0_s = jnp.transpose(state_ssm, (0, 1, 4, 2, 3)).reshape(depth, bs, HEAD_DIM, GROUP_WIDTH)
    pb_p = jnp.zeros((bp, 16, GROUP_WIDTH), F32)
    pb_s = _pad_rows_front(state_pool, 16)
    page_flat = page_table.reshape(-1).astype(jnp.int32)
    n_pool = cache_k.shape[1]
    cache_kt = jnp.transpose(cache_k, (0, 1, 3, 4, 5, 2)).reshape(depth, n_pool, GROUP_WIDTH, PAGE_SIZE)
    cache_v2 = cache_v.reshape(depth, n_pool, PAGE_SIZE * ATTN_HEADS, 2 * HEAD_DIM)
    eye_hc = jnp.eye(2 * ATTN_HEADS, dtype=BF16)

    hp = x_prompt.reshape(bp * lp, d)
    hs = x_sample.reshape(bs * ls, d)
    tm_p = 512 if (bp * lp) % 512 == 0 and lp % 512 == 0 else lp
    tm_s = bs * ls
    tf = ffn // 2 if (ffn // 2) % 128 == 0 else ffn

    outs_p = [[], [], [], [], []]
    outs_s = [[], [], [], [], []]

    def state_outputs(k, v, ncx, ncbc, hout, nbuf, n_seq, seq_len, store):
        store[0].append(k.reshape(n_seq, seq_len, ATTN_HEADS, 2, HEAD_DIM))
        store[1].append(v.reshape(n_seq, seq_len, ATTN_HEADS, 2 * HEAD_DIM))
        store[2].append(jnp.concatenate([ncx, ncbc], axis=-1)[:, 8 - (CONV_TAPS - 1):, :])
        store[3].append(jnp.transpose(hout.reshape(n_seq, HEAD_DIM, SSD_HEADS, HEAD_DIM), (0, 2, 3, 1)))
        store[4].append(nbuf[:, 16 - POOL_BUF:, :])

    for l in range(depth):
        ssd_w = (cwx, cwbc, cbiasx, cbiasbc, dtb, alog, dskip, ssd_gain, head_expand)
        z, xs, pin, qb, k, v, kb, vb, bc, dt = _in_proj(l, hp, norm_mix3, w_in_p, cos_p, sin_p, q_gain, k_gain,
                                                        seg_mat, tm_p, n_seq=bp)
        k = jnp.transpose(k.reshape(bp, ATTN_HEADS, 2, HEAD_DIM, lp), (0, 4, 1, 2, 3))
        y_ssd, ncx, ncbc, hout = _ssd(l, z, xs, bc, dt, zc[..., :GROUP_WIDTH], zc[..., GROUP_WIDTH:], h0_p,
                                      *ssd_w, n_seq=bp, seq_len=lp)
        y_pool, nbuf = _pool(l, pin, pb_p, pool_w_b, pool_scale3, bp, lp, 0)
        y_attn = _prompt_attn(l, safe_flags, qb, kb, vb, lam_p, subln3, bp, lp)
        hp = _out_ffn(l, hp, y_ssd, y_pool, y_attn, w_out_b, norm_ffn3, w_gu_b, w_down_b, tm_p, tf)
        state_outputs(k, v, ncx, ncbc, hout, nbuf, bp, lp, outs_p)

        z, xs, pin, qb, k, v, kb, vb, bc, dt = _in_proj(l, hs, norm_mix3, w_in_p, cos_s, sin_s, q_gain, k_gain,
                                                        seg_mat, tm_s)
        y_ssd, ncx, ncbc, hout = _ssd(l, z, xs, bc, dt, cb_s[l, ..., :GROUP_WIDTH], cb_s[l, ..., GROUP_WIDTH:],
                                      h0_s[l], *ssd_w, n_seq=bs, seq_len=ls)
        y_pool, nbuf = _pool(l, pin, pb_s[l], pool_w_b, pool_scale3, bs, ls, past_len)
        q4 = qb.reshape(bs, ls, 2 * ATTN_HEADS, HEAD_DIM)
        qrows = jnp.transpose(q4[:, :, :, None, :] * eye_hc[None, None, :, :, None], (0, 2, 1, 3, 4))
        qrows = qrows.reshape(bs, 2 * ATTN_HEADS * ls, GROUP_WIDTH)
        knew_t = jnp.transpose(k.reshape(bs, ls, GROUP_WIDTH), (0, 2, 1))
        knew_t = jnp.concatenate([knew_t, jnp.zeros((bs, GROUP_WIDTH, PAGE_SIZE - ls), F32)], axis=2)
        vnew = v.reshape(bs, ls * ATTN_HEADS, 2 * HEAD_DIM)
        vnew = jnp.concatenate([vnew, jnp.zeros((bs, (PAGE_SIZE - ls) * ATTN_HEADS, 2 * HEAD_DIM), F32)], axis=1)
        y_attn = _decode_attn(l, page_flat, qrows, cache_kt, cache_v2, knew_t, vnew, lam_p, subln3,
                              bs, ls, n_pages)
        hs = _out_ffn(l, hs, y_ssd, y_pool, y_attn, w_out_b, norm_ffn3, w_gu_b, w_down_b, tm_s, tf)
        state_outputs(k, v, ncx, ncbc, hout, nbuf, bs, ls, outs_s)

    stacked_p = [jnp.stack(s, axis=0) for s in outs_p]
    stacked_s = [jnp.stack(s, axis=0) for s in outs_s]
    return (hp.reshape(bp, lp, d), hs.reshape(bs, ls, d), *stacked_p, *stacked_s)
```
